```python
import jax, jax.numpy as jnp
from jax import lax
import numpy as np

D_MODEL = 1024
BATCH = 8
SEQ = 2048
DEPTH = 2
DEC_BATCH = 32
DEC_SEQ = 4
PAST_LEN = 16384
PAGE_SIZE = 128

EXPAND = 2
D_A = EXPAND * D_MODEL // 2
CONV_WIDTH = 31
HEAD_DIM_B = 64
N_HEADS_B = (EXPAND * D_MODEL // 2) // HEAD_DIM_B
D_B = N_HEADS_B * HEAD_DIM_B
DSWA_CONFIGS = ((128, 1), (512, 4), (2048, 16))
DSWA_MAX_WINDOW = 2048
ROPE_THETA = 10000.0
HGRN_HEAD_K = 128
HGRN_HEAD_V = 128
N_HEADS_C = EXPAND * D_MODEL // HGRN_HEAD_V
D_CK = N_HEADS_C * HGRN_HEAD_K
D_C = N_HEADS_C * HGRN_HEAD_V
HGRN_CHUNK = 64
N_EVEN = (DEPTH + 1) // 2
N_ODD = DEPTH // 2
NORM_EPS = 1e-6
D_IN_AB = 3 * D_A + 4 * D_B
D_IN_C = 2 * D_CK + 2 * D_C

kernel_name = 'hybrid_conv_dilswa_hgrn2_step'


def rms_norm(x, g):
    xf = x.astype(jnp.float32)
    y = xf * lax.rsqrt(jnp.mean(xf * xf, axis=-1, keepdims=True) + NORM_EPS)
    return (y * g.astype(jnp.float32)).astype(x.dtype)


def layer_norm(x, g, b):
    xf = x.astype(jnp.float32)
    xc = xf - jnp.mean(xf, axis=-1, keepdims=True)
    y = xc * lax.rsqrt(jnp.mean(xc * xc, axis=-1, keepdims=True) + NORM_EPS)
    return (y * g.astype(jnp.float32) + b.astype(jnp.float32)).astype(x.dtype)


def rope(x, pos):
    half = x.shape[-1] // 2
    inv_freq = ROPE_THETA ** (-jnp.arange(half, dtype=jnp.float32) / half)
    ang = pos.astype(jnp.float32)[:, None] * inv_freq[None, :]
    cos = jnp.cos(ang)[None, :, None, :]
    sin = jnp.sin(ang)[None, :, None, :]
    xf = x.astype(jnp.float32)
    x1, x2 = xf[..., :half], xf[..., half:]
    return jnp.concatenate([x1 * cos - x2 * sin, x2 * cos + x1 * sin], axis=-1).astype(x.dtype)


def conformer_conv(a_val, a_gate, conv_buf, conv_w, conv_b, ln_g, ln_b):
    u = a_val * jax.nn.sigmoid(a_gate)
    u_ext = jnp.concatenate([conv_buf.astype(u.dtype), u], axis=1)
    y = lax.conv_general_dilated(
        u_ext, conv_w[:, None, :].astype(u.dtype), window_strides=(1,), padding='VALID',
        dimension_numbers=('NWC', 'WIO', 'NWC'), feature_group_count=u.shape[-1])
    y = y + conv_b
    y = jax.nn.silu(layer_norm(y, ln_g, ln_b))
    return y, u_ext[:, -(CONV_WIDTH - 1):]


def dswa_prompt(q, k, v, window, dilation):
    n, s, h, hd = q.shape
    span = window // dilation
    blk = span
    n_sub = s // dilation
    nb = -(-n_sub // blk)
    lp = nb * blk

    def split(t):
        t = t.reshape(n, n_sub, dilation, h, hd).transpose(0, 2, 1, 3, 4)
        t = jnp.pad(t, ((0, 0), (0, 0), (0, lp - n_sub), (0, 0), (0, 0)))
        return t.reshape(n, dilation, nb, blk, h, hd)

    def with_prev(t):
        prev = jnp.pad(t, ((0, 0), (0, 0), (1, 0), (0, 0), (0, 0), (0, 0)))[:, :, :-1]
        return jnp.concatenate([prev, t], axis=3)

    qb = split(q)
    kk = with_prev(split(k))
    vv = with_prev(split(v))
    scores = jnp.einsum('ndbqhe,ndbkhe->ndbhqk', qb, kk).astype(jnp.float32) * (hd ** -0.5)
    qi = jnp.arange(nb)[:, None] * blk + jnp.arange(blk)[None, :]
    ki = jnp.arange(nb)[:, None] * blk - blk + jnp.arange(2 * blk)[None, :]
    dist = qi[:, :, None] - ki[:, None, :]
    mask = (dist >= 0) & (dist <= span) & (ki[:, None, :] >= 0)
    scores = jnp.where(mask[None, None, :, None, :, :], scores, -jnp.inf)
    lse = jax.nn.logsumexp(scores, axis=-1)
    p = jnp.exp(scores - lse[..., None])
    o = jnp.einsum('ndbhqk,ndbkhe->ndbqhe', p.astype(vv.dtype), vv)
    o = o.reshape(n, dilation, lp, h, hd)[:, :, :n_sub].transpose(0, 2, 1, 3, 4).reshape(n, s, h, hd)
    lse = lse.transpose(0, 1, 2, 4, 3).reshape(n, dilation, lp, h)[:, :, :n_sub]
    lse = lse.transpose(0, 2, 1, 3).reshape(n, s, h)
    return o, lse


def dswa_sample(q, k_all, v_all, n_past, window, dilation):
    t = q.shape[1]
    span = window // dilation
    idx = n_past + jnp.arange(t)[:, None] - dilation * jnp.arange(span + 1)[None, :]
    valid = idx >= 0
    idx = jnp.maximum(idx, 0)
    kg = k_all[:, idx]
    vg = v_all[:, idx]
    scores = jnp.einsum('nthe,ntkhe->nthk', q, kg).astype(jnp.float32) * (q.shape[-1] ** -0.5)
    scores = jnp.where(valid[None, :, None, :], scores, -jnp.inf)
    lse = jax.nn.logsumexp(scores, axis=-1)
    p = jnp.exp(scores - lse[..., None])
    o = jnp.einsum('nthk,ntkhe->nthe', p.astype(vg.dtype), vg)
    return o, lse


def dswa_merge(parts):
    outs = jnp.stack([o for o, _ in parts], axis=0).astype(jnp.float32)
    lses = jnp.stack([l for _, l in parts], axis=0)
    w = jax.nn.softmax(lses, axis=0)
    return jnp.einsum('cnth,cnthe->nthe', w, outs)


def hgrn2_recurrence(q, k, v, logf, s0):
    n, t, h, dk = q.shape
    dv = v.shape[-1]
    c = min(HGRN_CHUNK, t)
    nc = -(-t // c)
    pad = nc * c - t

    def blocks(a):
        a = jnp.pad(a.astype(jnp.float32), ((0, 0), (0, pad), (0, 0), (0, 0)))
        return a.reshape(n, nc, c, h, a.shape[-1]).transpose(1, 0, 3, 2, 4)

    causal = jnp.tril(jnp.ones((c, c), dtype=bool))

    def step(s, inp):
        qc, kc, vc, gc = inp
        b = jnp.cumsum(gc, axis=2)
        expo = b[:, :, :, None, :] - b[:, :, None, :, :]
        decay = jnp.exp(jnp.where(causal[None, None, :, :, None], expo, -jnp.inf))
        attn = jnp.einsum('nhtd,nhsd,nhtsd->nhts', qc, kc, decay)
        o = jnp.einsum('nhts,nhsv->nhtv', attn, vc) + jnp.einsum('nhtd,nhdv->nhtv', qc * jnp.exp(b), s)
        b_last = b[:, :, -1]
        s_new = jnp.exp(b_last)[..., None] * s + jnp.einsum(
            'nhsd,nhsv->nhdv', kc * jnp.exp(b_last[:, :, None, :] - b), vc)
        return s_new, o

    s_fin, o = lax.scan(step, s0.astype(jnp.float32), (blocks(q), blocks(k), blocks(v), blocks(logf)))
    o = o.transpose(1, 0, 3, 2, 4).reshape(n, nc * c, h, dv)[:, :t]
    return o, s_fin


def even_layer(x, pos, conv_buf, kv_cache, pre_g, post_g, w_in, w_out, conv_w, conv_b, ln_g, ln_b):
    n, t, _ = x.shape
    h = rms_norm(x, pre_g)
    proj = jnp.einsum('ntd,de->nte', h, w_in)
    a_val, a_gate, a_out_gate, qb, kb, vb, b_gate = jnp.split(
        proj, [D_A, 2 * D_A, 3 * D_A, 3 * D_A + D_B, 3 * D_A + 2 * D_B, 3 * D_A + 3 * D_B], axis=-1)
    ya, conv_new = conformer_conv(a_val, a_gate, conv_buf, conv_w, conv_b, ln_g, ln_b)
    ya = ya * jax.nn.silu(a_out_gate)
    q = rope(qb.reshape(n, t, N_HEADS_B, HEAD_DIM_B), pos)
    k = rope(kb.reshape(n, t, N_HEADS_B, HEAD_DIM_B), pos)
    v = vb.reshape(n, t, N_HEADS_B, HEAD_DIM_B)
    if kv_cache is None:
        parts = [dswa_prompt(q, k, v, wnd, dil) for (wnd, dil) in DSWA_CONFIGS]
    else:
        k_cache, v_cache = kv_cache
        n_past = k_cache.shape[1]
        k_all = jnp.concatenate([k_cache.astype(k.dtype), k], axis=1)
        v_all = jnp.concatenate([v_cache.astype(v.dtype), v], axis=1)
        parts = [dswa_sample(q, k_all, v_all, n_past, wnd, dil) for (wnd, dil) in DSWA_CONFIGS]
    yb = dswa_merge(parts).astype(x.dtype).reshape(n, t, D_B) * jax.nn.silu(b_gate)
    y = jnp.einsum('nte,ed->ntd', jnp.concatenate([ya, yb], axis=-1), w_out)
    return x + rms_norm(y, post_g), conv_new, k, v


def odd_layer(x, s0, lb, pre_g, post_g, w_in, w_out, g_norm):
    n, t, _ = x.shape
    h = rms_norm(x, pre_g)
    proj = jnp.einsum('ntd,de->nte', h, w_in)
    qc, fc, ic, gate = jnp.split(proj, [D_CK, 2 * D_CK, 2 * D_CK + D_C], axis=-1)
    q = jax.nn.silu(qc).reshape(n, t, N_HEADS_C, HGRN_HEAD_K)
    f = lb + (1.0 - lb) * jax.nn.sigmoid(fc.astype(jnp.float32))
    logf = jnp.log(f).reshape(n, t, N_HEADS_C, HGRN_HEAD_K)
    k = (1.0 - f).reshape(n, t, N_HEADS_C, HGRN_HEAD_K)
    v = ic.reshape(n, t, N_HEADS_C, HGRN_HEAD_V)
    o, s_new = hgrn2_recurrence(q, k, v, logf, s0)
    o = rms_norm(o.astype(x.dtype), g_norm).reshape(n, t, D_C) * jax.nn.silu(gate)
    y = jnp.einsum('nte,ed->ntd', o, w_out)
    return x + rms_norm(y, post_g), s_new


def _normal(key, shape, scale):
    return scale * jax.random.normal(key, shape, jnp.float32)


def setup_inputs(seed: int = 0) -> dict:
    key = jax.random.key(seed)
    ks = jax.random.split(key, 18)
    cache_len = min(DSWA_MAX_WINDOW, PAST_LEN)
    return {
        'x_prompt': _normal(ks[0], (BATCH, SEQ, D_MODEL), 1.0),
        'x_sample': _normal(ks[1], (DEC_BATCH, DEC_SEQ, D_MODEL), 1.0),
        'cache_conv': _normal(ks[2], (N_EVEN, DEC_BATCH, CONV_WIDTH - 1, D_A), 0.5),
        'cache_swa_k': _normal(ks[3], (N_EVEN, DEC_BATCH, cache_len, N_HEADS_B, HEAD_DIM_B), 1.0),
        'cache_swa_v': _normal(ks[4], (N_EVEN, DEC_BATCH, cache_len, N_HEADS_B, HEAD_DIM_B), 1.0),
        'state_hgrn': _normal(ks[5], (N_ODD, DEC_BATCH, N_HEADS_C, HGRN_HEAD_K, HGRN_HEAD_V), 0.5),
        'pre_norm': 1.0 + _normal(ks[6], (DEPTH, D_MODEL), 0.05),
        'post_norm': 1.0 + _normal(ks[7], (DEPTH, D_MODEL), 0.05),
        'w_in_ab': _normal(ks[8], (N_EVEN, D_MODEL, D_IN_AB), D_MODEL ** -0.5),
        'w_out_ab': _normal(ks[9], (N_EVEN, D_A + D_B, D_MODEL), (D_A + D_B) ** -0.5),
        'conv_w': _normal(ks[10], (N_EVEN, CONV_WIDTH, D_A), CONV_WIDTH ** -0.5),
        'conv_b': _normal(ks[11], (N_EVEN, D_A), 0.02),
        'conv_ln_g': 1.0 + _normal(ks[12], (N_EVEN, D_A), 0.05),
        'conv_ln_b': _normal(ks[13], (N_EVEN, D_A), 0.02),
        'w_in_c': _normal(ks[14], (N_ODD, D_MODEL, D_IN_C), D_MODEL ** -0.5),
        'w_out_c': _normal(ks[15], (N_ODD, D_C, D_MODEL), D_C ** -0.5),
        'hgrn_gnorm': 1.0 + _normal(ks[16], (N_ODD, HGRN_HEAD_V), 0.05),
        'hgrn_lb': _normal(ks[17], (DEPTH, D_CK), 0.1),
    }


def reference(x_prompt, x_sample, cache_conv, cache_swa_k, cache_swa_v, state_hgrn,
              pre_norm, post_norm, w_in_ab, w_out_ab, conv_w, conv_b, conv_ln_g, conv_ln_b,
              w_in_c, w_out_c, hgrn_gnorm, hgrn_lb):
    n_p, seq_p, _ = x_prompt.shape
    n_s, seq_s, _ = x_sample.shape
    pos_p = jnp.arange(seq_p, dtype=jnp.int32)
    pos_s = PAST_LEN + jnp.arange(seq_s, dtype=jnp.int32)
    lb_table = jnp.cumsum(jax.nn.softmax(hgrn_lb.astype(jnp.float32), axis=0), axis=0)
    lb_table = lb_table - lb_table[0]
    keep = min(DSWA_MAX_WINDOW, seq_p)
    yp, ys = x_prompt, x_sample
    conv_p, conv_s, kp_l, vp_l, ks_l, vs_l, sp_l, ss_l = [], [], [], [], [], [], [], []
    for layer in range(DEPTH):
        if layer % 2 == 0:
            e = layer // 2
            prm = (pre_norm[layer], post_norm[layer], w_in_ab[e], w_out_ab[e],
                   conv_w[e], conv_b[e], conv_ln_g[e], conv_ln_b[e])
            zero_buf = jnp.zeros((n_p, CONV_WIDTH - 1, D_A), yp.dtype)
            yp, cb_p, k_p, v_p = even_layer(yp, pos_p, zero_buf, None, *prm)
            ys, cb_s, k_s, v_s = even_layer(ys, pos_s, cache_conv[e], (cache_swa_k[e], cache_swa_v[e]), *prm)
            conv_p.append(cb_p)
            conv_s.append(cb_s)
            kp_l.append(k_p[:, -keep:])
            vp_l.append(v_p[:, -keep:])
            ks_l.append(k_s)
            vs_l.append(v_s)
        else:
            o = layer // 2
            prm = (pre_norm[layer], post_norm[layer], w_in_c[o], w_out_c[o], hgrn_gnorm[o])
            s0 = jnp.zeros((n_p, N_HEADS_C, HGRN_HEAD_K, HGRN_HEAD_V), jnp.float32)
            yp, s_p = odd_layer(yp, s0, lb_table[layer], *prm)
            ys, s_s = odd_layer(ys, state_hgrn[o], lb_table[layer], *prm)
            sp_l.append(s_p)
            ss_l.append(s_s)
    new_conv_prompt = jnp.stack(conv_p, axis=0)
    new_conv_sample = jnp.stack(conv_s, axis=0)
    new_k_prompt = jnp.stack(kp_l, axis=0)
    new_v_prompt = jnp.stack(vp_l, axis=0)
    new_k_sample = jnp.stack(ks_l, axis=0)
    new_v_sample = jnp.stack(vs_l, axis=0)
    new_state_prompt = jnp.stack(sp_l, axis=0)
    new_state_sample = jnp.stack(ss_l, axis=0)
    return (yp, ys, new_conv_prompt, new_conv_sample, new_k_prompt, new_v_prompt,
            new_k_sample, new_v_sample, new_state_prompt, new_state_sample)
```

```python
import functools

import jax
import jax.numpy as jnp
from jax import lax
from jax.experimental import pallas as pl
from jax.experimental.pallas import tpu as pltpu

F32 = jnp.float32
BF16 = jnp.bfloat16

D_MODEL = 1024
D_A = 1024
D_B = 1024
CONV_WIDTH = 31
HALO = 32
HEAD_DIM_B = 64
N_HEADS_B = 16
LANES = 128
N_PAIRS = D_B // LANES
DSWA_CONFIGS = ((128, 1), (512, 4), (2048, 16))
SPAN = 128
ROPE_THETA = 10000.0
PAST_LEN = 16384
HGRN_HEAD = 128
N_HEADS_C = 16
D_C = N_HEADS_C * HGRN_HEAD
NORM_EPS = 1e-6
NEG = -1e30
SAMPLE_PAD = 16
VMEM_LIMIT = 56 * 1024 * 1024


def _silu(x):
    return x * jax.nn.sigmoid(x)


def _params(*sem):
    return pltpu.CompilerParams(dimension_semantics=sem, vmem_limit_bytes=VMEM_LIMIT)


def _const_spec(shape):
    return pl.BlockSpec(shape, lambda *_: (0,) * len(shape))


def _rope_table_kernel(pos_ref, invf_ref, sign_ref, cos_ref, sin_ref):
    ang = pos_ref[...] * invf_ref[...]
    cos_ref[...] = jnp.cos(ang)
    sin_ref[...] = jnp.sin(ang) * sign_ref[...]


def _rope_tables(pos):
    t = pos.shape[0]
    half = HEAD_DIM_B // 2
    inv_freq = ROPE_THETA ** (-jnp.arange(half, dtype=F32) / half)
    invf = jnp.tile(inv_freq, LANES // half)[None, :]
    lane = jnp.arange(LANES)
    sign = jnp.where(lane % HEAD_DIM_B < half, -1.0, 1.0).astype(F32)[None, :]
    posb = jnp.broadcast_to(pos.astype(F32)[:, None], (t, LANES))
    return pl.pallas_call(
        _rope_table_kernel,
        out_shape=(jax.ShapeDtypeStruct((t, LANES), F32),) * 2,
        name="rope_tables",
    )(posb, invf, sign)


def _even_inproj_kernel(x_ref, g_ref, w_ref, cos_ref, sin_ref,
                        u_ref, ga_ref, q_ref, k_ref, v_ref, gb_ref):
    x = x_ref[...]
    ms = jnp.mean(x * x, axis=-1, keepdims=True)
    h = (x * lax.rsqrt(ms + NORM_EPS) * g_ref[...]).astype(BF16)

    def seg(j):
        return jnp.dot(h, w_ref[:, j * D_A:(j + 1) * D_A], preferred_element_type=F32)

    u_ref[...] = seg(0) * jax.nn.sigmoid(seg(1))
    ga_ref[...] = _silu(seg(2)).astype(BF16)

    cos = cos_ref[...]
    sin = sin_ref[...]
    lane = lax.broadcasted_iota(jnp.int32, cos.shape, 1)
    first_half = (lane % HEAD_DIM_B) < (HEAD_DIM_B // 2)

    def rope(val, out_ref):
        for c in range(N_PAIRS):
            xc = val[:, c * LANES:(c + 1) * LANES]
            partner = jnp.where(first_half,
                                pltpu.roll(xc, LANES - HEAD_DIM_B // 2, 1),
                                pltpu.roll(xc, HEAD_DIM_B // 2, 1))
            out_ref[:, c * LANES:(c + 1) * LANES] = xc * cos + partner * sin

    rope(seg(3), q_ref)
    rope(seg(4), k_ref)
    v_ref[...] = seg(5)
    gb_ref[...] = _silu(seg(6)).astype(BF16)


def _even_inproj(x2d, pre_g, w_bf16, cos_t, sin_t, tm):
    rows = x2d.shape[0]
    nt_tab = cos_t.shape[0] // tm
    row_spec = pl.BlockSpec((tm, D_MODEL), lambda i: (i, 0))
    tab_spec = pl.BlockSpec((tm, LANES), lambda i: (i % nt_tab, 0))
    f32_out = jax.ShapeDtypeStruct((rows, D_A), F32)
    bf_out = jax.ShapeDtypeStruct((rows, D_A), BF16)
    return pl.pallas_call(
        _even_inproj_kernel,
        grid=(rows // tm,),
        in_specs=[row_spec, _const_spec((1, D_MODEL)),
                  pl.BlockSpec(w_bf16.shape, lambda i: (0, 0), pipeline_mode=pl.Buffered(1)),
                  tab_spec, tab_spec],
        out_specs=[row_spec] * 6,
        out_shape=(f32_out, bf_out, f32_out, f32_out, f32_out, bf_out),
        compiler_params=_params("arbitrary"),
        name="even_inproj",
    )(x2d, pre_g[None, :], w_bf16, cos_t, sin_t)


def _conv_kernel(u_ref, ga_ref, buf_ref, w_ref, cb_ref, lg_ref, lb_ref,
                 ya_ref, tail_ref, ext_ref, y_ref, *, tc, t_valid, nt):
    t = pl.program_id(1)

    @pl.when(t == 0)
    def _():
        ext_ref[0:HALO, :] = buf_ref[0]

    ext_ref[HALO:HALO + tc, :] = u_ref[0]
    first = HALO - (CONV_WIDTH - 1)
    for c in range(D_A // LANES):
        cols = slice(c * LANES, (c + 1) * LANES)
        acc = jnp.zeros((tc, LANES), F32)
        for k in range(CONV_WIDTH):
            acc = acc + ext_ref[first + k:first + k + tc, cols] * w_ref[k:k + 1, cols]
        y_ref[:, cols] = acc + cb_ref[:, cols]
    y = y_ref[...]
    yc = y - jnp.mean(y, axis=-1, keepdims=True)
    yn = yc * lax.rsqrt(jnp.mean(yc * yc, axis=-1, keepdims=True) + NORM_EPS)
    yn = yn * lg_ref[...] + lb_ref[...]
    ya_ref[0] = (_silu(yn) * ga_ref[0].astype(F32)).astype(BF16)

    @pl.when(t == nt - 1)
    def _():
        tail_ref[0] = ext_ref[t_valid:t_valid + HALO, :]

    if nt > 1:
        ext_ref[0:HALO, :] = ext_ref[tc:tc + HALO, :]


def _conv_module(u, ga, buf, conv_w, conv_b, ln_g, ln_b, tc, t_valid):
    n, t, _ = u.shape
    nt = t // tc
    blk = pl.BlockSpec((1, tc, D_A), lambda b, i: (b, i, 0))
    halo = pl.BlockSpec((1, HALO, D_A), lambda b, i: (b, 0, 0))
    last_valid = t_valid - (nt - 1) * tc
    return pl.pallas_call(
        functools.partial(_conv_kernel, tc=tc, t_valid=last_valid, nt=nt),
        grid=(n, nt),
        in_specs=[blk, blk, halo, _const_spec((CONV_WIDTH, D_A)),
                  _const_spec((1, D_A)), _const_spec((1, D_A)), _const_spec((1, D_A))],
        out_specs=[blk, halo],
        out_shape=(jax.ShapeDtypeStruct((n, t, D_A), BF16),
                   jax.ShapeDtypeStruct((n, HALO, D_A), F32)),
        scratch_shapes=[pltpu.VMEM((HALO + tc, D_A), F32), pltpu.VMEM((tc, D_A), F32)],
        compiler_params=_params("arbitrary", "arbitrary"),
        name="conv_module",
    )(u, ga, buf, conv_w, conv_b[None, :], ln_g[None, :], ln_b[None, :])


def _softmax_block(s, mask, v):
    s = jnp.where(mask, s, NEG)
    m = jnp.max(s, axis=-1, keepdims=True)
    p = jnp.exp(s - m)
    l = jnp.sum(p, axis=-1, keepdims=True)
    o = jnp.dot(p.astype(BF16), v, preferred_element_type=F32) / l
    return o, m + jnp.log(l)


def _qk(q, k):
    return lax.dot_general(q, k, (((1,), (1,)), ((), ())), preferred_element_type=F32)


def _attn_prompt_kernel(q_ref, k_ref, v_ref, gb_ref, out_ref,
                        qt0, qt1, kt, vt, qd0, qd1, kd, vd,
                        o1, l1, od, ld, mrg, *, seq):
    nres = DSWA_CONFIGS[2][1]
    nsub = seq // nres
    scale = HEAD_DIM_B ** -0.5
    lane_t = lax.broadcasted_iota(jnp.int32, (seq, LANES), 1)
    q_all = q_ref[0] * scale
    qt0[...] = jnp.where(lane_t < HEAD_DIM_B, q_all, 0.0).astype(BF16)
    qt1[...] = jnp.where(lane_t < HEAD_DIM_B, 0.0, q_all).astype(BF16)
    kt[...] = k_ref[0].astype(BF16)
    vt[...] = v_ref[0].astype(BF16)
    lo = lax.broadcasted_iota(jnp.int32, (nsub, LANES), 1) < HEAD_DIM_B
    for r in range(nres):
        qr = q_ref[0, pl.ds(r, nsub, stride=nres), :] * scale
        qd0[r] = jnp.where(lo, qr, 0.0).astype(BF16)
        qd1[r] = jnp.where(lo, 0.0, qr).astype(BF16)
        kd[r] = k_ref[0, pl.ds(r, nsub, stride=nres), :].astype(BF16)
        vd[r] = v_ref[0, pl.ds(r, nsub, stride=nres), :].astype(BF16)

    def two_heads(q0, q1, k, v, mask):
        oa, la = _softmax_block(_qk(q0, k), mask, v)
        ob, lb = _softmax_block(_qk(q1, k), mask, v)
        lo_q = lax.broadcasted_iota(jnp.int32, oa.shape, 1) < HEAD_DIM_B
        return jnp.where(lo_q, oa, ob), jnp.where(lo_q, la, lb)

    blk = SPAN
    row = lax.broadcasted_iota(jnp.int32, (blk, 2 * blk), 0)
    col = lax.broadcasted_iota(jnp.int32, (blk, 2 * blk), 1)

    def body1(g, carry):
        qs = pl.multiple_of(g * blk, blk)
        ks = pl.multiple_of(jnp.maximum(g * blk - blk, 0), blk)
        dist = row + (qs - ks) - col
        mask = jnp.abs(2 * dist - SPAN) <= SPAN
        o, l = two_heads(qt0[pl.ds(qs, blk), :], qt1[pl.ds(qs, blk), :],
                         kt[pl.ds(ks, 2 * blk), :], vt[pl.ds(ks, 2 * blk), :], mask)
        o1[pl.ds(qs, blk), :] = o
        l1[pl.ds(qs, blk), :] = l
        return carry

    lax.fori_loop(0, seq // blk, body1, 0)

    qrows = 32
    row2 = lax.broadcasted_iota(jnp.int32, (4 * qrows, 8 * qrows), 0)
    col2 = lax.broadcasted_iota(jnp.int32, (4 * qrows, 8 * qrows), 1)
    sub_q = 4 * (row2 % qrows) + row2 // qrows
    sub_k = 4 * (col2 % (2 * qrows)) + col2 // (2 * qrows)

    def body2(idx, carry):
        r4 = idx // 4
        g = idx % 4
        i_q = pl.multiple_of(g * qrows, qrows)
        i_k = pl.multiple_of(jnp.maximum(g * qrows - qrows, 0), qrows)
        dist = sub_q - sub_k + 4 * (i_q - i_k)
        mask = jnp.abs(2 * dist - SPAN) <= SPAN

        def gather(ref, start, size):
            return jnp.concatenate([ref[4 * j + r4, pl.ds(start, size), :] for j in range(4)], axis=0)

        o, l = two_heads(gather(qd0, i_q, qrows), gather(qd1, i_q, qrows),
                         gather(kd, i_k, 2 * qrows), gather(vd, i_k, 2 * qrows), mask)
        for j in range(4):
            od[0, 4 * j + r4, pl.ds(i_q, qrows), :] = o[j * qrows:(j + 1) * qrows]
            ld[0, 4 * j + r4, pl.ds(i_q, qrows), :] = l[j * qrows:(j + 1) * qrows]
        return carry

    lax.fori_loop(0, 16, body2, 0)

    row3 = lax.broadcasted_iota(jnp.int32, (nsub, nsub), 0)
    col3 = lax.broadcasted_iota(jnp.int32, (nsub, nsub), 1)
    causal = row3 >= col3

    def body3(r, carry):
        o, l = two_heads(qd0[r], qd1[r], kd[r], vd[r], causal)
        od[1, r] = o
        ld[1, r] = l
        return carry

    lax.fori_loop(0, nres, body3, 0)

    for r in range(nres):
        oa = o1[pl.ds(r, nsub, stride=nres), :]
        la = l1[pl.ds(r, nsub, stride=nres), :]
        ob, lb = od[0, r], ld[0, r]
        oc, lc = od[1, r], ld[1, r]
        mx = jnp.maximum(jnp.maximum(la, lb), lc)
        ea, eb, ec = jnp.exp(la - mx), jnp.exp(lb - mx), jnp.exp(lc - mx)
        mrg[pl.ds(r, nsub, stride=nres), :] = (ea * oa + eb * ob + ec * oc) / (ea + eb + ec)
    out_ref[0] = (mrg[...] * gb_ref[0].astype(F32)).astype(BF16)


def _attn_prompt(q, k, v, gb):
    n, seq, _ = q.shape
    assert seq == DSWA_CONFIGS[2][0] and seq % SPAN == 0
    nres = DSWA_CONFIGS[2][1]
    nsub = seq // nres
    blk = pl.BlockSpec((1, seq, LANES), lambda b, h: (b, 0, h))
    t_bf = pltpu.VMEM((seq, LANES), BF16)
    d_bf = pltpu.VMEM((nres, nsub, LANES), BF16)
    t_f32 = pltpu.VMEM((seq, LANES), F32)
    d_f32 = pltpu.VMEM((2, nres, nsub, LANES), F32)
    return pl.pallas_call(
        functools.partial(_attn_prompt_kernel, seq=seq),
        grid=(n, N_PAIRS),
        in_specs=[blk] * 4,
        out_specs=blk,
        out_shape=jax.ShapeDtypeStruct((n, seq, D_B), BF16),
        scratch_shapes=[t_bf] * 4 + [d_bf] * 4 + [t_f32, t_f32, d_f32, d_f32, t_f32],
        compiler_params=_params("arbitrary", "arbitrary"),
        name="attn_prompt",
    )(q, k, v, gb)


def _attn_sample_kernel(q_ref, kn_ref, vn_ref, ktail_ref, vtail_ref,
                        k30, k31, k32, k33, v30, v31, v32, v33, gb_ref,
                        out_ref, kext, vext, kc, vc, *, t_new, tail):
    nq = 8
    scale = HEAD_DIM_B ** -0.5
    kext[0:tail, :] = ktail_ref[0]
    kext[tail:tail + SAMPLE_PAD, :] = kn_ref[0]
    vext[0:tail, :] = vtail_ref[0]
    vext[tail:tail + SAMPLE_PAD, :] = vn_ref[0]
    nkeys = t_new * SPAN + LANES
    zeros = jnp.zeros((LANES - SAMPLE_PAD, LANES), BF16)
    kc[t_new * SPAN:t_new * SPAN + SAMPLE_PAD, :] = kn_ref[0].astype(BF16)
    vc[t_new * SPAN:t_new * SPAN + SAMPLE_PAD, :] = vn_ref[0].astype(BF16)
    kc[t_new * SPAN + SAMPLE_PAD:nkeys, :] = zeros
    vc[t_new * SPAN + SAMPLE_PAD:nkeys, :] = zeros

    q8 = q_ref[0, 0:nq, :] * scale
    lane = lax.broadcasted_iota(jnp.int32, (nq, LANES), 1)
    qm = jnp.concatenate([jnp.where(lane < HEAD_DIM_B, q8, 0.0),
                          jnp.where(lane < HEAD_DIM_B, 0.0, q8)], axis=0).astype(BF16)
    row = lax.broadcasted_iota(jnp.int32, (2 * nq, nkeys), 0) % nq
    col = lax.broadcasted_iota(jnp.int32, (2 * nq, nkeys), 1)
    mask = jnp.where(col < t_new * SPAN, col // SPAN, col - t_new * SPAN) == row
    lo = lane < HEAD_DIM_B

    def config(slabs_k, slabs_v):
        for t in range(t_new):
            kc[t * SPAN:(t + 1) * SPAN, :] = slabs_k[t].astype(BF16)
            vc[t * SPAN:(t + 1) * SPAN, :] = slabs_v[t].astype(BF16)
        o, l = _softmax_block(_qk(qm, kc[...]), mask, vc[...])
        l = jnp.broadcast_to(l, o.shape)
        return jnp.where(lo, o[0:nq], o[nq:2 * nq]), jnp.where(lo, l[0:nq], l[nq:2 * nq])

    d1, d2 = DSWA_CONFIGS[0][1], DSWA_CONFIGS[1][1]
    oa, la = config([kext[tail + t - SPAN * d1:tail + t, :] for t in range(t_new)],
                    [vext[tail + t - SPAN * d1:tail + t, :] for t in range(t_new)])
    ob, lb = config([kext[pl.ds(tail + t - SPAN * d2, SPAN, stride=d2), :] for t in range(t_new)],
                    [vext[pl.ds(tail + t - SPAN * d2, SPAN, stride=d2), :] for t in range(t_new)])
    oc, lc = config([k30[0], k31[0], k32[0], k33[0]], [v30[0], v31[0], v32[0], v33[0]])
    mx = jnp.maximum(jnp.maximum(la, lb), lc)
    ea, eb, ec = jnp.exp(la - mx), jnp.exp(lb - mx), jnp.exp(lc - mx)
    merged = (ea * oa + eb * ob + ec * oc) / (ea + eb + ec)
    gated = merged * gb_ref[0].astype(F32)[0:nq]
    out_ref[0] = jnp.concatenate([gated, jnp.zeros((SAMPLE_PAD - nq, LANES), F32)], axis=0).astype(BF16)


def _attn_sample(q, k_new, v_new, k_cache, v_cache, gb, t_new):
    n, n_past, _ = k_cache.shape
    w3, d3 = DSWA_CONFIGS[2]
    w2, d2 = DSWA_CONFIGS[1]
    tail = w2
    assert n_past == w3 and n_past % d3 == 0 and t_new == 4 and t_new <= d2
    k_res = k_cache.reshape(n, n_past // d3, d3 * D_B)
    v_res = v_cache.reshape(n, n_past // d3, d3 * D_B)
    new_blk = pl.BlockSpec((1, SAMPLE_PAD, LANES), lambda b, h: (b, 0, h))
    tail_blk = pl.BlockSpec((1, tail, LANES), lambda b, h: (b, n_past // tail - 1, h))

    def res_blk(r):
        return pl.BlockSpec((1, n_past // d3, LANES), lambda b, h: (b, 0, r * N_PAIRS + h))

    return pl.pallas_call(
        functools.partial(_attn_sample_kernel, t_new=t_new, tail=tail),
        grid=(n, N_PAIRS),
        in_specs=[new_blk] * 3 + [tail_blk] * 2 + [res_blk(r) for r in range(t_new)] * 2 + [new_blk],
        out_specs=new_blk,
        out_shape=jax.ShapeDtypeStruct((n, SAMPLE_PAD, D_B), BF16),
        scratch_shapes=[pltpu.VMEM((tail + SAMPLE_PAD, LANES), F32)] * 2
        + [pltpu.VMEM((t_new * SPAN + LANES, LANES), BF16)] * 2,
        compiler_params=_params("arbitrary", "arbitrary"),
        name="attn_sample",
    )(q, k_new, v_new, k_cache, v_cache, *([k_res] * t_new), *([v_res] * t_new), gb)


def _outproj_kernel(*refs, n_in):
    ins, (w_ref, g_ref, x_ref, out_ref) = refs[:n_in], refs[n_in:]
    y = None
    off = 0
    for r in ins:
        width = r.shape[1]
        part = jnp.dot(r[...], w_ref[off:off + width, :], preferred_element_type=F32)
        y = part if y is None else y + part
        off += width
    yn = y * lax.rsqrt(jnp.mean(y * y, axis=-1, keepdims=True) + NORM_EPS) * g_ref[...]
    out_ref[...] = x_ref[...] + yn


def _outproj(parts, w_bf16, post_g, x2d, tm):
    rows = x2d.shape[0]
    row_spec = pl.BlockSpec((tm, D_MODEL), lambda i: (i, 0))
    return pl.pallas_call(
        functools.partial(_outproj_kernel, n_in=len(parts)),
        grid=(rows // tm,),
        in_specs=[pl.BlockSpec((tm, p.shape[1]), lambda i: (i, 0)) for p in parts]
        + [pl.BlockSpec(w_bf16.shape, lambda i: (0, 0), pipeline_mode=pl.Buffered(1)),
           _const_spec((1, D_MODEL)), row_spec],
        out_specs=row_spec,
        out_shape=jax.ShapeDtypeStruct((rows, D_MODEL), F32),
        compiler_params=_params("arbitrary"),
        name="outproj",
    )(*parts, w_bf16, post_g[None, :], x2d)


def _odd_inproj_kernel(x_ref, g_ref, w_ref, lbp_ref, q_ref, lf_ref, k_ref, v_ref, gt_ref, *, layer):
    x = x_ref[...]
    ms = jnp.mean(x * x, axis=-1, keepdims=True)
    h = (x * lax.rsqrt(ms + NORM_EPS) * g_ref[...]).astype(BF16)

    def seg(j):
        return jnp.dot(h, w_ref[:, j * D_C:(j + 1) * D_C], preferred_element_type=F32)

    lbp = lbp_ref[...]
    e = jnp.exp(lbp - jnp.max(lbp, axis=0, keepdims=True))
    sm = e / jnp.sum(e, axis=0, keepdims=True)
    lb = jnp.sum(sm[1:layer + 1], axis=0, keepdims=True)

    q_ref[...] = _silu(seg(0)).astype(BF16)
    f = lb + (1.0 - lb) * jax.nn.sigmoid(seg(1))
    lf_ref[...] = jnp.log(f)
    k_ref[...] = (1.0 - f).astype(BF16)
    v_ref[...] = seg(2).astype(BF16)
    gt_ref[...] = _silu(seg(3)).astype(BF16)


def _odd_inproj(x2d, pre_g, w_bf16, hgrn_lb, layer, tm):
    rows = x2d.shape[0]
    row_spec = pl.BlockSpec((tm, D_MODEL), lambda i: (i, 0))
    wide = pl.BlockSpec((tm, D_C), lambda i: (i, 0))
    bf = jax.ShapeDtypeStruct((rows, D_C), BF16)
    return pl.pallas_call(
        functools.partial(_odd_inproj_kernel, layer=layer),
        grid=(rows // tm,),
        in_specs=[row_spec, _const_spec((1, D_MODEL)),
                  pl.BlockSpec(w_bf16.shape, lambda i: (0, 0), pipeline_mode=pl.Buffered(1)),
                  _const_spec(hgrn_lb.shape)],
        out_specs=[wide] * 5,
        out_shape=(bf, jax.ShapeDtypeStruct((rows, D_C), F32), bf, bf, bf),
        compiler_params=_params("arbitrary"),
        name="odd_inproj",
    )(x2d, pre_g[None, :], w_bf16, hgrn_lb)


DIAG = 8


def _hgrn_kernel(q_ref, lf_ref, k_ref, v_ref, gt_ref, gn_ref, s0_ref,
                 o_ref, sfin_ref, st_s, b_s, bp_s, kp_s, vp_s, *, tile, t_valid, nt, hb):
    t = pl.program_id(2)

    @pl.when(t == 0)
    def _():
        for h in range(hb):
            st_s[h] = s0_ref[0, h].T

    row = lax.broadcasted_iota(jnp.int32, (tile, LANES), 0)
    rr = lax.broadcasted_iota(jnp.int32, (tile, tile), 0)
    cc = lax.broadcasted_iota(jnp.int32, (tile, tile), 1)
    tri = (rr >= cc).astype(BF16)
    zpad = jnp.zeros((DIAG, LANES), F32)
    levels = []
    m = DIAG
    while 2 * m <= tile:
        levels.append(m)
        m *= 2

    for h in range(hb):
        cols = slice(h * LANES, (h + 1) * LANES)
        lf = lf_ref[0, :, cols]
        kf = k_ref[0, :, cols].astype(F32)
        qf = q_ref[0, :, cols].astype(F32)
        vb = v_ref[0, :, cols]
        if t_valid < tile:
            lf = jnp.where(row < t_valid, lf, 0.0)
            kf = jnp.where(row < t_valid, kf, 0.0)
        p1 = lf.astype(BF16)
        r1 = lf - p1.astype(F32)
        p2 = r1.astype(BF16)
        p3 = (r1 - p2.astype(F32)).astype(BF16)
        b = (jnp.dot(tri, p1, preferred_element_type=F32)
             + jnp.dot(tri, p2, preferred_element_type=F32)
             + jnp.dot(tri, p3, preferred_element_type=F32))
        b_s[...] = b
        bp_s[0:DIAG, :] = zpad
        kp_s[0:DIAG, :] = zpad
        vp_s[0:DIAG, :] = zpad
        bp_s[DIAG:DIAG + tile, :] = b
        kp_s[DIAG:DIAG + tile, :] = kf
        vp_s[DIAG:DIAG + tile, :] = vb.astype(F32)

        attn = jnp.zeros((tile, tile), F32)
        for m in levels:
            nb = tile // (2 * m)
            if nb > 1:
                mid = b_s[pl.ds(m - 1, nb, stride=2 * m), :]
                mid = jnp.broadcast_to(mid[:, None, :], (nb, 2 * m, LANES)).reshape(tile, LANES)
            else:
                mid = jnp.broadcast_to(b_s[m - 1:m, :], (tile, LANES))
            second = (row % (2 * m)) >= m
            ql = (qf * jnp.exp(jnp.where(second, b - mid, NEG))).astype(BF16)
            kl = (kf * jnp.exp(jnp.where(second, NEG, mid - b))).astype(BF16)
            attn = attn + jnp.where(rr // (2 * m) == cc // (2 * m), _qk(ql, kl), 0.0)
        o = jnp.dot(attn.astype(BF16), vb, preferred_element_type=F32)

        for j in range(DIAG):
            bj = bp_s[DIAG - j:DIAG - j + tile, :]
            kj = kp_s[DIAG - j:DIAG - j + tile, :]
            vj = vp_s[DIAG - j:DIAG - j + tile, :]
            e = jnp.where((row % DIAG) >= j, b - bj, NEG)
            w = jnp.sum(qf * kj * jnp.exp(e), axis=-1, keepdims=True)
            o = o + w * vj

        st = st_s[h]
        o = o + _qk((qf * jnp.exp(b)).astype(BF16), st.astype(BF16))
        b_last = b_s[tile - 1:tile, :]
        kdec = (kf * jnp.exp(b_last - b)).astype(BF16)
        st_s[h] = jnp.exp(b_last) * st + lax.dot_general(
            vb, kdec, (((0,), (0,)), ((), ())), preferred_element_type=F32)

        on = o * lax.rsqrt(jnp.mean(o * o, axis=-1, keepdims=True) + NORM_EPS) * gn_ref[...]
        o_ref[0, :, cols] = (on * gt_ref[0, :, cols].astype(F32)).astype(BF16)

    @pl.when(t == nt - 1)
    def _():
        for h in range(hb):
            sfin_ref[0, h] = st_s[h].T


def _hgrn(q, lf, k, v, gt, g_norm, s0, tile, t_valid, hb):
    n, t, _ = q.shape
    nt = t // tile
    blk = pl.BlockSpec((1, tile, hb * LANES), lambda b, h, i: (b, i, h))
    st_blk = pl.BlockSpec((1, hb, HGRN_HEAD, HGRN_HEAD), lambda b, h, i: (b, h, 0, 0))
    pad = pltpu.VMEM((DIAG + tile, LANES), F32)
    return pl.pallas_call(
        functools.partial(_hgrn_kernel, tile=tile, t_valid=t_valid, nt=nt, hb=hb),
        grid=(n, N_HEADS_C // hb, nt),
        in_specs=[blk] * 5 + [_const_spec((1, LANES)), st_blk],
        out_specs=[blk, st_blk],
        out_shape=(jax.ShapeDtypeStruct((n, t, D_C), BF16),
                   jax.ShapeDtypeStruct((n, N_HEADS_C, HGRN_HEAD, HGRN_HEAD), F32)),
        scratch_shapes=[pltpu.VMEM((hb, HGRN_HEAD, HGRN_HEAD), F32),
                        pltpu.VMEM((tile, LANES), F32), pad, pad, pad],
        compiler_params=_params("arbitrary", "arbitrary", "arbitrary"),
        name="hgrn",
    )(q, lf, k, v, gt, g_norm[None, :], s0)


def _trunk(x, pos, conv_hist, kv_cache, state0, t_valid, prm, tiles):
    n, t, _ = x.shape
    rows = n * t
    tm, tc, th, hb = tiles
    x2d = x.reshape(rows, D_MODEL)
    cos_t, sin_t = _rope_tables(pos)
    if cos_t.shape[0] < tm:
        cos_t = jnp.tile(cos_t, (tm // t, 1))
        sin_t = jnp.tile(sin_t, (tm // t, 1))
    u, ga, q, k, v, gb = _even_inproj(x2d, prm["pre0"], prm["w_in_ab"], cos_t, sin_t, tm)
    shp = (n, t, D_A)
    ya, hist = _conv_module(u.reshape(shp), ga.reshape(shp), conv_hist, prm["conv_w"], prm["conv_b"],
                            prm["ln_g"], prm["ln_b"], tc, t_valid)
    if kv_cache is None:
        yb = _attn_prompt(q.reshape(shp), k.reshape(shp), v.reshape(shp), gb.reshape(shp))
    else:
        yb = _attn_sample(q.reshape(shp), k.reshape(shp), v.reshape(shp), kv_cache[0], kv_cache[1],
                          gb.reshape(shp), t_valid)
    x1 = _outproj([ya.reshape(rows, D_A), yb.reshape(rows, D_B)], prm["w_out_ab"], prm["post0"], x2d, tm)
    qc, lf, kc, vc, gt = _odd_inproj(x1, prm["pre1"], prm["w_in_c"], prm["hgrn_lb"], 1, tm)
    wide = (n, t, D_C)
    oc, s_fin = _hgrn(qc.reshape(wide), lf.reshape(wide), kc.reshape(wide), vc.reshape(wide),
                      gt.reshape(wide), prm["gnorm"], state0, th, t_valid, hb)
    x2 = _outproj([oc.reshape(rows, D_C)], prm["w_out_c"], prm["post1"], x1, tm)
    return x2.reshape(n, t, D_MODEL), hist, k.reshape(shp), v.reshape(shp), s_fin


def kernel(x_prompt, x_sample, cache_conv, cache_swa_k, cache_swa_v, state_hgrn, pre_norm, post_norm,
           w_in_ab, w_out_ab, conv_w, conv_b, conv_ln_g, conv_ln_b, w_in_c, w_out_c, hgrn_gnorm, hgrn_lb):
    n_p, seq_p, _ = x_prompt.shape
    n_s, seq_s, _ = x_sample.shape
    hist_rows = CONV_WIDTH - 1
    prm = dict(pre0=pre_norm[0], pre1=pre_norm[1], post0=post_norm[0], post1=post_norm[1],
               w_in_ab=w_in_ab[0].astype(BF16), w_out_ab=w_out_ab[0].astype(BF16),
               conv_w=conv_w[0], conv_b=conv_b[0], ln_g=conv_ln_g[0], ln_b=conv_ln_b[0],
               w_in_c=w_in_c[0].astype(BF16), w_out_c=w_out_c[0].astype(BF16),
               gnorm=hgrn_gnorm[0], hgrn_lb=hgrn_lb)

    pos_p = jnp.arange(seq_p, dtype=jnp.int32)
    yp, hist_p, k_p, v_p, s_p = _trunk(
        x_prompt, pos_p, jnp.zeros((n_p, HALO, D_A), F32), None,
        jnp.zeros((n_p, N_HEADS_C, HGRN_HEAD, HGRN_HEAD), F32), seq_p, prm, (256, 256, 256, 2))

    pad = SAMPLE_PAD - seq_s
    xs = jnp.pad(x_sample, ((0, 0), (0, pad), (0, 0)))
    pos_s = PAST_LEN + jnp.arange(SAMPLE_PAD, dtype=jnp.int32)
    hist_s = jnp.pad(cache_conv[0], ((0, 0), (HALO - hist_rows, 0), (0, 0)))
    n_past = cache_swa_k.shape[2]
    kv = (cache_swa_k[0].reshape(n_s, n_past, D_B), cache_swa_v[0].reshape(n_s, n_past, D_B))
    ys, hist_s, k_s, v_s, s_s = _trunk(
        xs, pos_s, hist_s, kv, state_hgrn[0], seq_s, prm, (256, SAMPLE_PAD, SAMPLE_PAD, N_HEADS_C))

    heads = (N_HEADS_B, HEAD_DIM_B)
    return (yp, ys[:, :seq_s],
            hist_p[None, :, HALO - hist_rows:], hist_s[None, :, HALO - hist_rows:],
            k_p.reshape(1, n_p, seq_p, *heads), v_p.reshape(1, n_p, seq_p, *heads),
            k_s[:, :seq_s].reshape(1, n_s, seq_s, *heads), v_s[:, :seq_s].reshape(1, n_s, seq_s, *heads),
            s_p[None], s_s[None])
```

```python
import functools

import jax
import jax.numpy as jnp
from jax import lax
from jax.experimental import pallas as pl
from jax.experimental.pallas import tpu as pltpu

F32 = jnp.float32
BF16 = jnp.bfloat16

D_MODEL = 1024
D_A = 1024
D_B = 1024
CONV_WIDTH = 31
HALO = 32
HEAD_DIM_B = 64
N_HEADS_B = 16
LANES = 128
N_PAIRS = D_B // LANES
DSWA_CONFIGS = ((128, 1), (512, 4), (2048, 16))
SPAN = 128
ROPE_THETA = 10000.0
PAST_LEN = 16384
HGRN_HEAD = 128
N_HEADS_C = 16
D_C = N_HEADS_C * HGRN_HEAD
NORM_EPS = 1e-6
NEG = -1e30
SAMPLE_PAD = 16
VMEM_LIMIT = 56 * 1024 * 1024


def _silu(x):
    return x * jax.nn.sigmoid(x)


def _params(*sem):
    return pltpu.CompilerParams(dimension_semantics=sem, vmem_limit_bytes=VMEM_LIMIT)


def _const_spec(shape):
    return pl.BlockSpec(shape, lambda *_: (0,) * len(shape))


def _rope_table_kernel(pos_ref, invf_ref, sign_ref, cos_ref, sin_ref):
    ang = pos_ref[...] * invf_ref[...]
    cos_ref[...] = jnp.cos(ang)
    sin_ref[...] = jnp.sin(ang) * sign_ref[...]


def _rope_tables(pos):
    t = pos.shape[0]
    half = HEAD_DIM_B // 2
    inv_freq = ROPE_THETA ** (-jnp.arange(half, dtype=F32) / half)
    invf = jnp.tile(inv_freq, LANES // half)[None, :]
    lane = jnp.arange(LANES)
    sign = jnp.where(lane % HEAD_DIM_B < half, -1.0, 1.0).astype(F32)[None, :]
    posb = jnp.broadcast_to(pos.astype(F32)[:, None], (t, LANES))
    return pl.pallas_call(
        _rope_table_kernel,
        out_shape=(jax.ShapeDtypeStruct((t, LANES), F32),) * 2,
        name="rope_tables",
    )(posb, invf, sign)


def _even_inproj_kernel(x_ref, g_ref, w_ref, cos_ref, sin_ref,
                        u_ref, ga_ref, q_ref, k_ref, v_ref, gb_ref, *kv_t_refs):
    x = x_ref[...]
    ms = jnp.mean(x * x, axis=-1, keepdims=True)
    h = (x * lax.rsqrt(ms + NORM_EPS) * g_ref[...]).astype(BF16)

    def seg(j):
        return jnp.dot(h, w_ref[:, j * D_A:(j + 1) * D_A], preferred_element_type=F32)

    u_ref[...] = seg(0) * jax.nn.sigmoid(seg(1))
    ga_ref[...] = _silu(seg(2)).astype(BF16)

    cos = cos_ref[...]
    sin = sin_ref[...]
    lane = lax.broadcasted_iota(jnp.int32, cos.shape, 1)
    first_half = (lane % HEAD_DIM_B) < (HEAD_DIM_B // 2)

    def emit(val, out_ref, rotate, t_ref):
        for c in range(N_PAIRS):
            xc = val[:, c * LANES:(c + 1) * LANES]
            if rotate:
                partner = jnp.where(first_half,
                                    pltpu.roll(xc, LANES - HEAD_DIM_B // 2, 1),
                                    pltpu.roll(xc, HEAD_DIM_B // 2, 1))
                xc = xc * cos + partner * sin
            out_ref[:, c * LANES:(c + 1) * LANES] = xc
            if t_ref is not None:
                xt = xc.T
                t_ref[0, 2 * c] = xt[0:HEAD_DIM_B]
                t_ref[0, 2 * c + 1] = xt[HEAD_DIM_B:LANES]

    kt_ref, vt_ref = kv_t_refs if kv_t_refs else (None, None)
    emit(seg(3), q_ref, True, None)
    emit(seg(4), k_ref, True, kt_ref)
    emit(seg(5), v_ref, False, vt_ref)
    gb_ref[...] = _silu(seg(6)).astype(BF16)


def _even_inproj(x2d, pre_g, w_bf16, cos_t, sin_t, tm, seq_t=None):
    rows = x2d.shape[0]
    nt_tab = cos_t.shape[0] // tm
    row_spec = pl.BlockSpec((tm, D_MODEL), lambda i: (i, 0))
    tab_spec = pl.BlockSpec((tm, LANES), lambda i: (i % nt_tab, 0))
    f32_out = jax.ShapeDtypeStruct((rows, D_A), F32)
    bf_out = jax.ShapeDtypeStruct((rows, D_A), BF16)
    out_specs = [row_spec] * 6
    out_shape = (f32_out, bf_out, f32_out, f32_out, f32_out, bf_out)
    if seq_t is not None:
        nt = seq_t // tm
        t_spec = pl.BlockSpec((1, N_HEADS_B, HEAD_DIM_B, tm), lambda i: (i // nt, 0, 0, i % nt))
        t_out = jax.ShapeDtypeStruct((rows // seq_t, N_HEADS_B, HEAD_DIM_B, seq_t), F32)
        out_specs = out_specs + [t_spec] * 2
        out_shape = out_shape + (t_out,) * 2
    return pl.pallas_call(
        _even_inproj_kernel,
        grid=(rows // tm,),
        in_specs=[row_spec, _const_spec((1, D_MODEL)),
                  pl.BlockSpec(w_bf16.shape, lambda i: (0, 0), pipeline_mode=pl.Buffered(1)),
                  tab_spec, tab_spec],
        out_specs=out_specs,
        out_shape=out_shape,
        compiler_params=_params("arbitrary"),
        name="even_inproj",
    )(x2d, pre_g[None, :], w_bf16, cos_t, sin_t)


def _conv_kernel(u_ref, ga_ref, buf_ref, w_ref, cb_ref, lg_ref, lb_ref,
                 ya_ref, tail_ref, ext_ref, y_ref, *, tc, t_valid, nt):
    t = pl.program_id(1)

    @pl.when(t == 0)
    def _():
        ext_ref[0:HALO, :] = buf_ref[0]

    ext_ref[HALO:HALO + tc, :] = u_ref[0]
    first = HALO - (CONV_WIDTH - 1)
    for c in range(D_A // LANES):
        cols = slice(c * LANES, (c + 1) * LANES)
        acc = jnp.zeros((tc, LANES), F32)
        for k in range(CONV_WIDTH):
            acc = acc + ext_ref[first + k:first + k + tc, cols] * w_ref[k:k + 1, cols]
        y_ref[:, cols] = acc + cb_ref[:, cols]
    y = y_ref[...]
    yc = y - jnp.mean(y, axis=-1, keepdims=True)
    yn = yc * lax.rsqrt(jnp.mean(yc * yc, axis=-1, keepdims=True) + NORM_EPS)
    yn = yn * lg_ref[...] + lb_ref[...]
    ya_ref[0] = (_silu(yn) * ga_ref[0].astype(F32)).astype(BF16)

    @pl.when(t == nt - 1)
    def _():
        tail_ref[0] = ext_ref[t_valid:t_valid + HALO, :]

    if nt > 1:
        ext_ref[0:HALO, :] = ext_ref[tc:tc + HALO, :]


def _conv_module(u, ga, buf, conv_w, conv_b, ln_g, ln_b, tc, t_valid):
    n, t, _ = u.shape
    nt = t // tc
    blk = pl.BlockSpec((1, tc, D_A), lambda b, i: (b, i, 0))
    halo = pl.BlockSpec((1, HALO, D_A), lambda b, i: (b, 0, 0))
    last_valid = t_valid - (nt - 1) * tc
    return pl.pallas_call(
        functools.partial(_conv_kernel, tc=tc, t_valid=last_valid, nt=nt),
        grid=(n, nt),
        in_specs=[blk, blk, halo, _const_spec((CONV_WIDTH, D_A)),
                  _const_spec((1, D_A)), _const_spec((1, D_A)), _const_spec((1, D_A))],
        out_specs=[blk, halo],
        out_shape=(jax.ShapeDtypeStruct((n, t, D_A), BF16),
                   jax.ShapeDtypeStruct((n, HALO, D_A), F32)),
        scratch_shapes=[pltpu.VMEM((HALO + tc, D_A), F32), pltpu.VMEM((tc, D_A), F32)],
        compiler_params=_params("arbitrary", "arbitrary"),
        name="conv_module",
    )(u, ga, buf, conv_w, conv_b[None, :], ln_g[None, :], ln_b[None, :])


def _softmax_block(s, mask, v):
    s = jnp.where(mask, s, NEG)
    m = jnp.max(s, axis=-1, keepdims=True)
    p = jnp.exp(s - m)
    l = jnp.sum(p, axis=-1, keepdims=True)
    o = jnp.dot(p.astype(BF16), v, preferred_element_type=F32) / l
    return o, m + jnp.log(l)


def _qk(q, k):
    return lax.dot_general(q, k, (((1,), (1,)), ((), ())), preferred_element_type=F32)


def _attn_prompt_kernel(q_ref, k_ref, v_ref, gb_ref, out_ref,
                        qt0, qt1, kt, vt, qd0, qd1, kd, vd,
                        o1, l1, od, ld, mrg, *, seq):
    nres = DSWA_CONFIGS[2][1]
    nsub = seq // nres
    scale = HEAD_DIM_B ** -0.5
    lane_t = lax.broadcasted_iota(jnp.int32, (seq, LANES), 1)
    q_all = q_ref[0] * scale
    qt0[...] = jnp.where(lane_t < HEAD_DIM_B, q_all, 0.0).astype(BF16)
    qt1[...] = jnp.where(lane_t < HEAD_DIM_B, 0.0, q_all).astype(BF16)
    kt[...] = k_ref[0].astype(BF16)
    vt[...] = v_ref[0].astype(BF16)
    lo = lax.broadcasted_iota(jnp.int32, (nsub, LANES), 1) < HEAD_DIM_B
    for r in range(nres):
        qr = q_ref[0, pl.ds(r, nsub, stride=nres), :] * scale
        qd0[r] = jnp.where(lo, qr, 0.0).astype(BF16)
        qd1[r] = jnp.where(lo, 0.0, qr).astype(BF16)
        kd[r] = k_ref[0, pl.ds(r, nsub, stride=nres), :].astype(BF16)
        vd[r] = v_ref[0, pl.ds(r, nsub, stride=nres), :].astype(BF16)

    def two_heads(q0, q1, k, v, mask):
        oa, la = _softmax_block(_qk(q0, k), mask, v)
        ob, lb = _softmax_block(_qk(q1, k), mask, v)
        lo_q = lax.broadcasted_iota(jnp.int32, oa.shape, 1) < HEAD_DIM_B
        return jnp.where(lo_q, oa, ob), jnp.where(lo_q, la, lb)

    blk = SPAN
    row = lax.broadcasted_iota(jnp.int32, (blk, 2 * blk), 0)
    col = lax.broadcasted_iota(jnp.int32, (blk, 2 * blk), 1)

    def body1(g, carry):
        qs = pl.multiple_of(g * blk, blk)
        ks = pl.multiple_of(jnp.maximum(g * blk - blk, 0), blk)
        dist = row + (qs - ks) - col
        mask = jnp.abs(2 * dist - SPAN) <= SPAN
        o, l = two_heads(qt0[pl.ds(qs, blk), :], qt1[pl.ds(qs, blk), :],
                         kt[pl.ds(ks, 2 * blk), :], vt[pl.ds(ks, 2 * blk), :], mask)
        o1[pl.ds(qs, blk), :] = o
        l1[pl.ds(qs, blk), :] = l
        return carry

    lax.fori_loop(0, seq // blk, body1, 0, unroll=2)

    qrows = 32
    row2 = lax.broadcasted_iota(jnp.int32, (4 * qrows, 8 * qrows), 0)
    col2 = lax.broadcasted_iota(jnp.int32, (4 * qrows, 8 * qrows), 1)
    sub_q = 4 * (row2 % qrows) + row2 // qrows
    sub_k = 4 * (col2 % (2 * qrows)) + col2 // (2 * qrows)

    def body2(idx, carry):
        r4 = idx // 4
        g = idx % 4
        i_q = pl.multiple_of(g * qrows, qrows)
        i_k = pl.multiple_of(jnp.maximum(g * qrows - qrows, 0), qrows)
        dist = sub_q - sub_k + 4 * (i_q - i_k)
        mask = jnp.abs(2 * dist - SPAN) <= SPAN

        def gather(ref, start, size):
            return jnp.concatenate([ref[4 * j + r4, pl.ds(start, size), :] for j in range(4)], axis=0)

        o, l = two_heads(gather(qd0, i_q, qrows), gather(qd1, i_q, qrows),
                         gather(kd, i_k, 2 * qrows), gather(vd, i_k, 2 * qrows), mask)
        for j in range(4):
            od[0, 4 * j + r4, pl.ds(i_q, qrows), :] = o[j * qrows:(j + 1) * qrows]
            ld[0, 4 * j + r4, pl.ds(i_q, qrows), :] = l[j * qrows:(j + 1) * qrows]
        return carry

    lax.fori_loop(0, 16, body2, 0, unroll=2)

    row3 = lax.broadcasted_iota(jnp.int32, (nsub, nsub), 0)
    col3 = lax.broadcasted_iota(jnp.int32, (nsub, nsub), 1)
    causal = row3 >= col3

    def body3(r, carry):
        o, l = two_heads(qd0[r], qd1[r], kd[r], vd[r], causal)
        od[1, r] = o
        ld[1, r] = l
        return carry

    lax.fori_loop(0, nres, body3, 0, unroll=2)

    for r in range(nres):
        oa = o1[pl.ds(r, nsub, stride=nres), :]
        la = l1[pl.ds(r, nsub, stride=nres), :]
        ob, lb = od[0, r], ld[0, r]
        oc, lc = od[1, r], ld[1, r]
        mx = jnp.maximum(jnp.maximum(la, lb), lc)
        ea, eb, ec = jnp.exp(la - mx), jnp.exp(lb - mx), jnp.exp(lc - mx)
        mrg[pl.ds(r, nsub, stride=nres), :] = (ea * oa + eb * ob + ec * oc) / (ea + eb + ec)
    out_ref[0] = (mrg[...] * gb_ref[0].astype(F32)).astype(BF16)


def _attn_prompt(q, k, v, gb):
    n, seq, _ = q.shape
    assert seq == DSWA_CONFIGS[2][0] and seq % SPAN == 0
    nres = DSWA_CONFIGS[2][1]
    nsub = seq // nres
    blk = pl.BlockSpec((1, seq, LANES), lambda b, h: (b, 0, h))
    t_bf = pltpu.VMEM((seq, LANES), BF16)
    d_bf = pltpu.VMEM((nres, nsub, LANES), BF16)
    t_f32 = pltpu.VMEM((seq, LANES), F32)
    d_f32 = pltpu.VMEM((2, nres, nsub, LANES), F32)
    return pl.pallas_call(
        functools.partial(_attn_prompt_kernel, seq=seq),
        grid=(n, N_PAIRS),
        in_specs=[blk] * 4,
        out_specs=blk,
        out_shape=jax.ShapeDtypeStruct((n, seq, D_B), BF16),
        scratch_shapes=[t_bf] * 4 + [d_bf] * 4 + [t_f32, t_f32, d_f32, d_f32, t_f32],
        compiler_params=_params("arbitrary", "arbitrary"),
        name="attn_prompt",
    )(q, k, v, gb)


def _attn_sample_kernel(q_ref, kn_ref, vn_ref, kt_ref, vt_ref, gb_ref, out_ref, *, t_new, n_past, heads):
    nq = SAMPLE_PAD
    width = heads * HEAD_DIM_B
    scale = HEAD_DIM_B ** -0.5
    q_all = q_ref[0] * scale
    k_new = kn_ref[0]
    v_new = vn_ref[0]

    lane = lax.broadcasted_iota(jnp.int32, (nq, width), 1)
    qm = jnp.concatenate([jnp.where(lane // HEAD_DIM_B == h, q_all, 0.0) for h in range(heads)], axis=0)
    s_new = _qk(qm.astype(BF16), k_new.astype(BF16))
    t_row = lax.broadcasted_iota(jnp.int32, (nq, nq), 0)
    t_col = lax.broadcasted_iota(jnp.int32, (nq, nq), 1)

    dist = (n_past + lax.broadcasted_iota(jnp.int32, (nq, n_past), 0)
            - lax.broadcasted_iota(jnp.int32, (nq, n_past), 1))
    far = 1 << 30
    masks, starts = [], []
    for window, dil in DSWA_CONFIGS:
        start = n_past - window
        d = dist[:, start:]
        masks.append(jnp.where((d & (dil - 1)) == 0, d, far) <= window)
        starts.append(start)

    outs = []
    for h in range(heads):
        cols = slice(h * HEAD_DIM_B, (h + 1) * HEAD_DIM_B)
        kt = kt_ref[0, h].astype(BF16)
        vt = vt_ref[0, h].astype(BF16)
        s = jnp.dot(q_all[:, cols].astype(BF16), kt, preferred_element_type=F32)
        sn = s_new[h * nq:(h + 1) * nq, :]
        vnh = v_new[:, cols]
        acc, den = [], []
        for (window, dil), mask, start in zip(DSWA_CONFIGS, masks, starts):
            new_ok = jnp.where(((t_row - t_col) & (dil - 1)) == 0, t_row - t_col, -1) >= 0
            sc = jnp.where(mask, s[:, start:], NEG)
            snm = jnp.where(new_ok, sn, NEG)
            m = jnp.maximum(jnp.max(sc, axis=-1, keepdims=True), jnp.max(snm, axis=-1, keepdims=True))
            p = jnp.exp(sc - m)
            pn = jnp.exp(snm - m)
            l = jnp.sum(p, axis=-1, keepdims=True) + jnp.sum(pn, axis=-1, keepdims=True)
            o = _qk(p.astype(BF16), vt[:, start:])
            for tn in range(t_new):
                o = o + pn[:, tn:tn + 1] * vnh[tn:tn + 1, :]
            acc.append(o / l)
            den.append(m + jnp.log(l))
        mx = jnp.maximum(jnp.maximum(den[0], den[1]), den[2])
        e = [jnp.exp(d - mx) for d in den]
        outs.append((e[0] * acc[0] + e[1] * acc[1] + e[2] * acc[2]) / (e[0] + e[1] + e[2]))
    merged = jnp.concatenate(outs, axis=1)
    out_ref[0] = (merged * gb_ref[0].astype(F32)).astype(BF16)


def _attn_sample(q, k_new, v_new, kt_cache, vt_cache, gb, t_new, heads=8):
    n, _, _, n_past = kt_cache.shape
    assert n_past >= DSWA_CONFIGS[2][0] and t_new <= min(d for _, d in DSWA_CONFIGS[1:]) and t_new <= SAMPLE_PAD
    width = heads * HEAD_DIM_B
    new_blk = pl.BlockSpec((1, SAMPLE_PAD, width), lambda b, h: (b, 0, h))
    cache_blk = pl.BlockSpec((1, heads, HEAD_DIM_B, n_past), lambda b, h: (b, h, 0, 0))
    return pl.pallas_call(
        functools.partial(_attn_sample_kernel, t_new=t_new, n_past=n_past, heads=heads),
        grid=(n, N_HEADS_B // heads),
        in_specs=[new_blk] * 3 + [cache_blk] * 2 + [new_blk],
        out_specs=new_blk,
        out_shape=jax.ShapeDtypeStruct((n, SAMPLE_PAD, D_B), BF16),
        compiler_params=_params("arbitrary", "arbitrary"),
        name="attn_sample",
    )(q, k_new, v_new, kt_cache, vt_cache, gb)


def _outproj_kernel(*refs, n_in):
    ins, (w_ref, g_ref, x_ref, out_ref) = refs[:n_in], refs[n_in:]
    y = None
    off = 0
    for r in ins:
        width = r.shape[1]
        part = jnp.dot(r[...], w_ref[off:off + width, :], preferred_element_type=F32)
        y = part if y is None else y + part
        off += width
    yn = y * lax.rsqrt(jnp.mean(y * y, axis=-1, keepdims=True) + NORM_EPS) * g_ref[...]
    out_ref[...] = x_ref[...] + yn


def _outproj(parts, w_bf16, post_g, x2d, tm):
    rows = x2d.shape[0]
    row_spec = pl.BlockSpec((tm, D_MODEL), lambda i: (i, 0))
    return pl.pallas_call(
        functools.partial(_outproj_kernel, n_in=len(parts)),
        grid=(rows // tm,),
        in_specs=[pl.BlockSpec((tm, p.shape[1]), lambda i: (i, 0)) for p in parts]
        + [pl.BlockSpec(w_bf16.shape, lambda i: (0, 0), pipeline_mode=pl.Buffered(1)),
           _const_spec((1, D_MODEL)), row_spec],
        out_specs=row_spec,
        out_shape=jax.ShapeDtypeStruct((rows, D_MODEL), F32),
        compiler_params=_params("arbitrary"),
        name="outproj",
    )(*parts, w_bf16, post_g[None, :], x2d)


def _odd_inproj_kernel(x_ref, g_ref, w_ref, lbp_ref, q_ref, lf_ref, k_ref, v_ref, gt_ref, *, layer):
    x = x_ref[...]
    ms = jnp.mean(x * x, axis=-1, keepdims=True)
    h = (x * lax.rsqrt(ms + NORM_EPS) * g_ref[...]).astype(BF16)

    def seg(j):
        return jnp.dot(h, w_ref[:, j * D_C:(j + 1) * D_C], preferred_element_type=F32)

    lbp = lbp_ref[...]
    e = jnp.exp(lbp - jnp.max(lbp, axis=0, keepdims=True))
    sm = e / jnp.sum(e, axis=0, keepdims=True)
    lb = jnp.sum(sm[1:layer + 1], axis=0, keepdims=True)

    q_ref[...] = _silu(seg(0)).astype(BF16)
    f = lb + (1.0 - lb) * jax.nn.sigmoid(seg(1))
    lf_ref[...] = jnp.log(f)
    k_ref[...] = (1.0 - f).astype(BF16)
    v_ref[...] = seg(2).astype(BF16)
    gt_ref[...] = _silu(seg(3)).astype(BF16)


def _odd_inproj(x2d, pre_g, w_bf16, hgrn_lb, layer, tm):
    rows = x2d.shape[0]
    row_spec = pl.BlockSpec((tm, D_MODEL), lambda i: (i, 0))
    wide = pl.BlockSpec((tm, D_C), lambda i: (i, 0))
    bf = jax.ShapeDtypeStruct((rows, D_C), BF16)
    return pl.pallas_call(
        functools.partial(_odd_inproj_kernel, layer=layer),
        grid=(rows // tm,),
        in_specs=[row_spec, _const_spec((1, D_MODEL)),
                  pl.BlockSpec(w_bf16.shape, lambda i: (0, 0), pipeline_mode=pl.Buffered(1)),
                  _const_spec(hgrn_lb.shape)],
        out_specs=[wide] * 5,
        out_shape=(bf, jax.ShapeDtypeStruct((rows, D_C), F32), bf, bf, bf),
        compiler_params=_params("arbitrary"),
        name="odd_inproj",
    )(x2d, pre_g[None, :], w_bf16, hgrn_lb)


DIAG = 8


def _hgrn_kernel(q_ref, lf_ref, k_ref, v_ref, gt_ref, gn_ref, s0_ref,
                 o_ref, sfin_ref, st_s, b_s, bp_s, kp_s, vp_s, *, tile, t_valid, nt, hb):
    t = pl.program_id(2)

    @pl.when(t == 0)
    def _():
        for h in range(hb):
            st_s[h] = s0_ref[0, h].T

    row = lax.broadcasted_iota(jnp.int32, (tile, LANES), 0)
    rr = lax.broadcasted_iota(jnp.int32, (tile, tile), 0)
    cc = lax.broadcasted_iota(jnp.int32, (tile, tile), 1)
    tri = (rr >= cc).astype(BF16)
    zpad = jnp.zeros((DIAG, LANES), F32)
    levels = []
    m = DIAG
    while 2 * m <= tile:
        levels.append(m)
        m *= 2

    for h in range(hb):
        cols = slice(h * LANES, (h + 1) * LANES)
        lf = lf_ref[0, :, cols]
        kf = k_ref[0, :, cols].astype(F32)
        qf = q_ref[0, :, cols].astype(F32)
        vb = v_ref[0, :, cols]
        if t_valid < tile:
            lf = jnp.where(row < t_valid, lf, 0.0)
            kf = jnp.where(row < t_valid, kf, 0.0)
        p1 = lf.astype(BF16)
        r1 = lf - p1.astype(F32)
        p2 = r1.astype(BF16)
        p3 = (r1 - p2.astype(F32)).astype(BF16)
        b = (jnp.dot(tri, p1, preferred_element_type=F32)
             + jnp.dot(tri, p2, preferred_element_type=F32)
             + jnp.dot(tri, p3, preferred_element_type=F32))
        b_s[...] = b
        bp_s[0:DIAG, :] = zpad
        kp_s[0:DIAG, :] = zpad
        vp_s[0:DIAG, :] = zpad
        bp_s[DIAG:DIAG + tile, :] = b
        kp_s[DIAG:DIAG + tile, :] = kf
        vp_s[DIAG:DIAG + tile, :] = vb.astype(F32)

        attn = jnp.zeros((tile, tile), F32)
        for m in levels:
            nb = tile // (2 * m)
            if nb > 1:
                mid = b_s[pl.ds(m - 1, nb, stride=2 * m), :]
                mid = jnp.broadcast_to(mid[:, None, :], (nb, 2 * m, LANES)).reshape(tile, LANES)
            else:
                mid = jnp.broadcast_to(b_s[m - 1:m, :], (tile, LANES))
            second = (row % (2 * m)) >= m
            ql = (qf * jnp.exp(jnp.where(second, b - mid, NEG))).astype(BF16)
            kl = (kf * jnp.exp(jnp.where(second, NEG, mid - b))).astype(BF16)
            attn = attn + jnp.where(rr // (2 * m) == cc // (2 * m), _qk(ql, kl), 0.0)
        o = jnp.dot(attn.astype(BF16), vb, preferred_element_type=F32)

        for j in range(DIAG):
            bj = bp_s[DIAG - j:DIAG - j + tile, :]
            kj = kp_s[DIAG - j:DIAG - j + tile, :]
            vj = vp_s[DIAG - j:DIAG - j + tile, :]
            e = jnp.where((row % DIAG) >= j, b - bj, NEG)
            w = jnp.sum(qf * kj * jnp.exp(e), axis=-1, keepdims=True)
            o = o + w * vj

        st = st_s[h]
        o = o + _qk((qf * jnp.exp(b)).astype(BF16), st.astype(BF16))
        b_last = b_s[tile - 1:tile, :]
        kdec = (kf * jnp.exp(b_last - b)).astype(BF16)
        st_s[h] = jnp.exp(b_last) * st + lax.dot_general(
            vb, kdec, (((0,), (0,)), ((), ())), preferred_element_type=F32)

        on = o * lax.rsqrt(jnp.mean(o * o, axis=-1, keepdims=True) + NORM_EPS) * gn_ref[...]
        o_ref[0, :, cols] = (on * gt_ref[0, :, cols].astype(F32)).astype(BF16)

    @pl.when(t == nt - 1)
    def _():
        for h in range(hb):
            sfin_ref[0, h] = st_s[h].T


def _hgrn(q, lf, k, v, gt, g_norm, s0, tile, t_valid, hb):
    n, t, _ = q.shape
    nt = t // tile
    blk = pl.BlockSpec((1, tile, hb * LANES), lambda b, h, i: (b, i, h))
    st_blk = pl.BlockSpec((1, hb, HGRN_HEAD, HGRN_HEAD), lambda b, h, i: (b, h, 0, 0))
    pad = pltpu.VMEM((DIAG + tile, LANES), F32)
    return pl.pallas_call(
        functools.partial(_hgrn_kernel, tile=tile, t_valid=t_valid, nt=nt, hb=hb),
        grid=(n, N_HEADS_C // hb, nt),
        in_specs=[blk] * 5 + [_const_spec((1, LANES)), st_blk],
        out_specs=[blk, st_blk],
        out_shape=(jax.ShapeDtypeStruct((n, t, D_C), BF16),
                   jax.ShapeDtypeStruct((n, N_HEADS_C, HGRN_HEAD, HGRN_HEAD), F32)),
        scratch_shapes=[pltpu.VMEM((hb, HGRN_HEAD, HGRN_HEAD), F32),
                        pltpu.VMEM((tile, LANES), F32), pad, pad, pad],
        compiler_params=_params("arbitrary", "arbitrary", "arbitrary"),
        name="hgrn",
    )(q, lf, k, v, gt, g_norm[None, :], s0)


def _trunk(x, pos, conv_hist, kv_cache, state0, t_valid, prm, tiles):
    n, t, _ = x.shape
    rows = n * t
    tm, tc, th, hb = tiles
    x2d = x.reshape(rows, D_MODEL)
    cos_t, sin_t = _rope_tables(pos)
    if cos_t.shape[0] < tm:
        cos_t = jnp.tile(cos_t, (tm // t, 1))
        sin_t = jnp.tile(sin_t, (tm // t, 1))
    kv_t = None
    if kv_cache is None:
        u, ga, q, k, v, gb, *kv_t = _even_inproj(x2d, prm["pre0"], prm["w_in_ab"], cos_t, sin_t, tm, seq_t=t)
    else:
        u, ga, q, k, v, gb = _even_inproj(x2d, prm["pre0"], prm["w_in_ab"], cos_t, sin_t, tm)
    shp = (n, t, D_A)
    ya, hist = _conv_module(u.reshape(shp), ga.reshape(shp), conv_hist, prm["conv_w"], prm["conv_b"],
                            prm["ln_g"], prm["ln_b"], tc, t_valid)
    if kv_cache is None:
        yb = _attn_prompt(q.reshape(shp), k.reshape(shp), v.reshape(shp), gb.reshape(shp))
    else:
        yb = _attn_sample(q.reshape(shp), k.reshape(shp), v.reshape(shp), kv_cache[0], kv_cache[1],
                          gb.reshape(shp), t_valid)
    x1 = _outproj([ya.reshape(rows, D_A), yb.reshape(rows, D_B)], prm["w_out_ab"], prm["post0"], x2d, tm)
    qc, lf, kc, vc, gt = _odd_inproj(x1, prm["pre1"], prm["w_in_c"], prm["hgrn_lb"], 1, tm)
    wide = (n, t, D_C)
    oc, s_fin = _hgrn(qc.reshape(wide), lf.reshape(wide), kc.reshape(wide), vc.reshape(wide),
                      gt.reshape(wide), prm["gnorm"], state0, th, t_valid, hb)
    x2 = _outproj([oc.reshape(rows, D_C)], prm["w_out_c"], prm["post1"], x1, tm)
    if kv_t is None:
        kv_t = (k.reshape(shp), v.reshape(shp))
    return x2.reshape(n, t, D_MODEL), hist, kv_t[0], kv_t[1], s_fin


def kernel(x_prompt, x_sample, cache_conv, cache_swa_k, cache_swa_v, state_hgrn, pre_norm, post_norm,
           w_in_ab, w_out_ab, conv_w, conv_b, conv_ln_g, conv_ln_b, w_in_c, w_out_c, hgrn_gnorm, hgrn_lb):
    n_p, seq_p, _ = x_prompt.shape
    n_s, seq_s, _ = x_sample.shape
    hist_rows = CONV_WIDTH - 1
    prm = dict(pre0=pre_norm[0], pre1=pre_norm[1], post0=post_norm[0], post1=post_norm[1],
               w_in_ab=w_in_ab[0].astype(BF16), w_out_ab=w_out_ab[0].astype(BF16),
               conv_w=conv_w[0], conv_b=conv_b[0], ln_g=conv_ln_g[0], ln_b=conv_ln_b[0],
               w_in_c=w_in_c[0].astype(BF16), w_out_c=w_out_c[0].astype(BF16),
               gnorm=hgrn_gnorm[0], hgrn_lb=hgrn_lb)

    pos_p = jnp.arange(seq_p, dtype=jnp.int32)
    yp, hist_p, k_p, v_p, s_p = _trunk(
        x_prompt, pos_p, jnp.zeros((n_p, HALO, D_A), F32), None,
        jnp.zeros((n_p, N_HEADS_C, HGRN_HEAD, HGRN_HEAD), F32), seq_p, prm, (256, 256, 256, 2))

    pad = SAMPLE_PAD - seq_s
    xs = jnp.pad(x_sample, ((0, 0), (0, pad), (0, 0)))
    pos_s = PAST_LEN + jnp.arange(SAMPLE_PAD, dtype=jnp.int32)
    hist_s = jnp.pad(cache_conv[0], ((0, 0), (HALO - hist_rows, 0), (0, 0)))
    kv = (jnp.transpose(cache_swa_k[0], (0, 2, 3, 1)), jnp.transpose(cache_swa_v[0], (0, 2, 3, 1)))
    ys, hist_s, k_s, v_s, s_s = _trunk(
        xs, pos_s, hist_s, kv, state_hgrn[0], seq_s, prm, (256, SAMPLE_PAD, SAMPLE_PAD, N_HEADS_C))

    heads = (N_HEADS_B, HEAD_DIM_B)
    return (yp, ys[:, :seq_s],
            hist_p[None, :, HALO - hist_rows:], hist_s[None, :, HALO - hist_rows:],
            jnp.transpose(k_p, (0, 3, 1, 2))[None], jnp.transpose(v_p, (0, 3, 1, 2))[None],
            k_s[:, :seq_s].reshape(1, n_s, seq_s, *heads), v_s[:, :seq_s].reshape(1, n_s, seq_s, *heads),
            s_p[None], s_s[None])
```

```python
import functools

import jax
import jax.numpy as jnp
from jax import lax
from jax.experimental import pallas as pl
from jax.experimental.pallas import tpu as pltpu

F32 = jnp.float32
BF16 = jnp.bfloat16

D_MODEL = 1024
D_A = 1024
D_B = 1024
CONV_WIDTH = 31
HALO = 32
HEAD_DIM_B = 64
N_HEADS_B = 16
LANES = 128
SUBLANES = 8
N_PAIRS = D_B // LANES
DSWA_CONFIGS = ((128, 1), (512, 4), (2048, 16))
SPAN = 128
ROPE_THETA = 10000.0
PAST_LEN = 16384
HGRN_HEAD = 128
N_HEADS_C = 16
D_C = N_HEADS_C * HGRN_HEAD
NORM_EPS = 1e-6
NEG = -1e30
LOG2_E = 1.4426950408889634
SAMPLE_PAD = 16
ATTN_GROUP = 4
VMEM_LIMIT = 56 * 1024 * 1024


def _silu(x):
    return x * jax.nn.sigmoid(x)


def _params(*sem):
    return pltpu.CompilerParams(dimension_semantics=sem, vmem_limit_bytes=VMEM_LIMIT)


def _const_spec(shape):
    return pl.BlockSpec(shape, lambda *_: (0,) * len(shape))


def _rope_table_kernel(pos_ref, invf_ref, sign_ref, cos_ref, sin_ref):
    ang = pos_ref[...] * invf_ref[...]
    cos_ref[...] = jnp.cos(ang)
    sin_ref[...] = jnp.sin(ang) * sign_ref[...]


def _rope_tables(pos):
    t = pos.shape[0]
    half = HEAD_DIM_B // 2
    inv_freq = ROPE_THETA ** (-jnp.arange(half, dtype=F32) / half)
    invf = jnp.tile(inv_freq, LANES // half)[None, :]
    lane = jnp.arange(LANES)
    sign = jnp.where(lane % HEAD_DIM_B < half, -1.0, 1.0).astype(F32)[None, :]
    posb = jnp.broadcast_to(pos.astype(F32)[:, None], (t, LANES))
    return pl.pallas_call(
        _rope_table_kernel,
        out_shape=(jax.ShapeDtypeStruct((t, LANES), F32),) * 2,
        name="rope_tables",
    )(posb, invf, sign)


def _even_inproj_kernel(x_ref, g_ref, w_ref, cos_ref, sin_ref,
                        u_ref, ga_ref, q_ref, k_ref, v_ref, gb_ref, *kv_t_refs):
    x = x_ref[...]
    ms = jnp.mean(x * x, axis=-1, keepdims=True)
    h = (x * lax.rsqrt(ms + NORM_EPS) * g_ref[...]).astype(BF16)

    def seg(j):
        return jnp.dot(h, w_ref[:, j * D_A:(j + 1) * D_A], preferred_element_type=F32)

    u_ref[...] = seg(0) * jax.nn.sigmoid(seg(1))
    ga_ref[...] = _silu(seg(2)).astype(BF16)

    cos = cos_ref[...]
    sin = sin_ref[...]
    lane = lax.broadcasted_iota(jnp.int32, cos.shape, 1)
    first_half = (lane % HEAD_DIM_B) < (HEAD_DIM_B // 2)

    def emit(val, out_ref, rotate, t_ref):
        for c in range(N_PAIRS):
            xc = val[:, c * LANES:(c + 1) * LANES]
            if rotate:
                partner = jnp.where(first_half,
                                    pltpu.roll(xc, LANES - HEAD_DIM_B // 2, 1),
                                    pltpu.roll(xc, HEAD_DIM_B // 2, 1))
                xc = xc * cos + partner * sin
            out_ref[:, c * LANES:(c + 1) * LANES] = xc
            if t_ref is not None:
                xt = xc.T
                t_ref[0, 2 * c] = xt[0:HEAD_DIM_B]
                t_ref[0, 2 * c + 1] = xt[HEAD_DIM_B:LANES]

    kt_ref, vt_ref = kv_t_refs if kv_t_refs else (None, None)
    emit(seg(3), q_ref, True, None)
    emit(seg(4), k_ref, True, kt_ref)
    emit(seg(5), v_ref, False, vt_ref)
    gb_ref[...] = _silu(seg(6)).astype(BF16)


def _even_inproj(x2d, pre_g, w_bf16, cos_t, sin_t, tm, seq_t=None):
    rows = x2d.shape[0]
    nt_tab = cos_t.shape[0] // tm
    row_spec = pl.BlockSpec((tm, D_MODEL), lambda i: (i, 0))
    tab_spec = pl.BlockSpec((tm, LANES), lambda i: (i % nt_tab, 0))
    f32_out = jax.ShapeDtypeStruct((rows, D_A), F32)
    bf_out = jax.ShapeDtypeStruct((rows, D_A), BF16)
    out_specs = [row_spec] * 6
    out_shape = (f32_out, bf_out, f32_out, f32_out, f32_out, bf_out)
    if seq_t is not None:
        nt = seq_t // tm
        t_spec = pl.BlockSpec((1, N_HEADS_B, HEAD_DIM_B, tm), lambda i: (i // nt, 0, 0, i % nt))
        t_out = jax.ShapeDtypeStruct((rows // seq_t, N_HEADS_B, HEAD_DIM_B, seq_t), F32)
        out_specs = out_specs + [t_spec] * 2
        out_shape = out_shape + (t_out,) * 2
    return pl.pallas_call(
        _even_inproj_kernel,
        grid=(rows // tm,),
        in_specs=[row_spec, _const_spec((1, D_MODEL)),
                  pl.BlockSpec(w_bf16.shape, lambda i: (0, 0), pipeline_mode=pl.Buffered(1)),
                  tab_spec, tab_spec],
        out_specs=out_specs,
        out_shape=out_shape,
        compiler_params=_params("arbitrary"),
        name="even_inproj",
    )(x2d, pre_g[None, :], w_bf16, cos_t, sin_t)


def _conv_kernel(u_ref, ga_ref, buf_ref, w_ref, cb_ref, lg_ref, lb_ref,
                 ya_ref, tail_ref, ext_ref, y_ref, part_ref, *, tc, t_valid, nt):
    t = pl.program_id(1)

    @pl.when(t == 0)
    def _():
        ext_ref[0:HALO, :] = buf_ref[0]
        ext_ref[HALO + tc:HALO + tc + SUBLANES, :] = jnp.zeros((SUBLANES, D_A), F32)

    ext_ref[HALO:HALO + tc, :] = u_ref[0]
    first = HALO - (CONV_WIDTH - 1)
    for c in range(D_A // LANES):
        cols = slice(c * LANES, (c + 1) * LANES)
        acc = jnp.zeros((tc, LANES), F32)
        for rho in range(SUBLANES):
            part = None
            for k in range(CONV_WIDTH):
                if (first + k) % SUBLANES == rho:
                    base = first + k - rho
                    term = ext_ref[base:base + tc + SUBLANES, cols] * w_ref[k:k + 1, cols]
                    part = term if part is None else part + term
            if part is None:
                continue
            if rho == 0:
                acc = acc + part[0:tc]
            else:
                part_ref[rho] = part
                acc = acc + part_ref[rho, rho:rho + tc, :]
        y_ref[:, cols] = acc + cb_ref[:, cols]
    y = y_ref[...]
    yc = y - jnp.mean(y, axis=-1, keepdims=True)
    yn = yc * lax.rsqrt(jnp.mean(yc * yc, axis=-1, keepdims=True) + NORM_EPS)
    yn = yn * lg_ref[...] + lb_ref[...]
    ya_ref[0] = (_silu(yn) * ga_ref[0].astype(F32)).astype(BF16)

    @pl.when(t == nt - 1)
    def _():
        tail_ref[0] = ext_ref[t_valid:t_valid + HALO, :]

    if nt > 1:
        ext_ref[0:HALO, :] = ext_ref[tc:tc + HALO, :]


def _conv_module(u, ga, buf, conv_w, conv_b, ln_g, ln_b, tc, t_valid):
    n, t, _ = u.shape
    nt = t // tc
    blk = pl.BlockSpec((1, tc, D_A), lambda b, i: (b, i, 0))
    halo = pl.BlockSpec((1, HALO, D_A), lambda b, i: (b, 0, 0))
    last_valid = t_valid - (nt - 1) * tc
    return pl.pallas_call(
        functools.partial(_conv_kernel, tc=tc, t_valid=last_valid, nt=nt),
        grid=(n, nt),
        in_specs=[blk, blk, halo, _const_spec((CONV_WIDTH, D_A)),
                  _const_spec((1, D_A)), _const_spec((1, D_A)), _const_spec((1, D_A))],
        out_specs=[blk, halo],
        out_shape=(jax.ShapeDtypeStruct((n, t, D_A), BF16),
                   jax.ShapeDtypeStruct((n, HALO, D_A), F32)),
        scratch_shapes=[pltpu.VMEM((HALO + tc + SUBLANES, D_A), F32), pltpu.VMEM((tc, D_A), F32),
                        pltpu.VMEM((SUBLANES, tc + SUBLANES, LANES), F32)],
        compiler_params=_params("arbitrary", "arbitrary"),
        name="conv_module",
    )(u, ga, buf, conv_w, conv_b[None, :], ln_g[None, :], ln_b[None, :])


def _qk(q, k):
    return lax.dot_general(q, k, (((1,), (1,)), ((), ())), preferred_element_type=F32)


def _attn_prompt_kernel(q_ref, k_ref, v_ref, gb_ref, out_ref, q_s, k_s, v_s, o_s, m_s, l_s, *, seq):
    nblk = seq // SPAN
    scale = HEAD_DIM_B ** -0.5 * LOG2_E
    lo = lax.broadcasted_iota(jnp.int32, (SPAN, LANES), 1) < HEAD_DIM_B

    def rows(c, blk):
        dil = DSWA_CONFIGS[c][1]
        per_seq = nblk // dil
        return dil * SPAN * (blk % per_seq) + blk // per_seq, dil

    def take(ref, start, stride):
        if stride == 1:
            return ref[0, start:start + SPAN, :]
        return ref[0, pl.ds(start, SPAN, stride=stride), :]

    for c in range(len(DSWA_CONFIGS)):
        for blk in range(nblk):
            start, stride = rows(c, blk)
            qb = take(q_ref, start, stride) * scale
            q_s[c, blk, 0:SPAN, :] = jnp.where(lo, qb, 0.0).astype(BF16)
            q_s[c, blk, SPAN:2 * SPAN, :] = jnp.where(lo, 0.0, qb).astype(BF16)
            k_s[c, blk] = take(k_ref, start, stride).astype(BF16)
            v_s[c, blk] = take(v_ref, start, stride).astype(BF16)

    row = lax.broadcasted_iota(jnp.int32, (2 * SPAN, SPAN), 0) % SPAN
    col = lax.broadcasted_iota(jnp.int32, (2 * SPAN, SPAN), 1)
    upper = col >= row
    diag = col == row
    lower = col < row

    def has_prev(c, blk):
        return blk % (nblk // DSWA_CONFIGS[c][1]) > 0

    def scores(c, blk):
        q2 = q_s[c, blk]
        s_own = _qk(q2, k_s[c, blk])
        s_far = _qk(q2, k_s[c, blk - 1]) if has_prev(c, blk) else NEG
        return s_own, s_far

    def probabilities(s_own, s_far):
        s_self = jnp.sum(jnp.where(diag, s_own, 0.0), axis=-1, keepdims=True)
        s = jnp.where(upper, s_far, s_own)
        m = jnp.maximum(jnp.max(s, axis=-1, keepdims=True), s_self)
        p = jnp.exp2(s - m)
        p_self = jnp.exp2(s_self - m)
        l = jnp.sum(p, axis=-1, keepdims=True) + p_self
        p_own = jnp.where(lower, p, jnp.where(diag, p_self, 0.0)).astype(BF16)
        p_far = jnp.where(upper, p, 0.0).astype(BF16)
        return p_own, p_far, m, l

    def finish(c, blk, p_own, p_far, m, l):
        o = jnp.dot(p_own, v_s[c, blk], preferred_element_type=F32)
        if has_prev(c, blk):
            o = o + jnp.dot(p_far, v_s[c, blk - 1], preferred_element_type=F32)
        m = jnp.broadcast_to(m, o.shape)
        l = jnp.broadcast_to(l, o.shape)
        start, stride = rows(c, blk)
        dst = pl.ds(start, SPAN, stride=stride) if stride > 1 else pl.ds(start, SPAN)
        o_s[c, dst, :] = jnp.where(lo, o[0:SPAN], o[SPAN:2 * SPAN])
        m_s[c, dst, :] = jnp.where(lo, m[0:SPAN], m[SPAN:2 * SPAN])
        l_s[c, dst, :] = jnp.where(lo, l[0:SPAN], l[SPAN:2 * SPAN])

    todo = [(c, blk) for c in range(len(DSWA_CONFIGS)) for blk in range(nblk)]
    groups = [todo[i:i + ATTN_GROUP] for i in range(0, len(todo), ATTN_GROUP)]
    pending = [scores(*cb) for cb in groups[0]]
    for gi, group in enumerate(groups):
        ahead = [scores(*cb) for cb in groups[gi + 1]] if gi + 1 < len(groups) else []
        probs = [probabilities(*sc) for sc in pending]
        for cb, pr in zip(group, probs):
            finish(*cb, *pr)
        pending = ahead

    ma, mb, mc = m_s[0], m_s[1], m_s[2]
    mx = jnp.maximum(jnp.maximum(ma, mb), mc)
    ea, eb, ec = jnp.exp2(ma - mx), jnp.exp2(mb - mx), jnp.exp2(mc - mx)
    merged = (ea * o_s[0] + eb * o_s[1] + ec * o_s[2]) / (ea * l_s[0] + eb * l_s[1] + ec * l_s[2])
    out_ref[0] = (merged * gb_ref[0].astype(F32)).astype(BF16)


def _attn_prompt(q, k, v, gb):
    n, seq, _ = q.shape
    nblk = seq // SPAN
    assert all(w // d == SPAN and nblk % d == 0 for w, d in DSWA_CONFIGS)
    ncfg = len(DSWA_CONFIGS)
    blk = pl.BlockSpec((1, seq, LANES), lambda b, h: (b, 0, h))
    return pl.pallas_call(
        functools.partial(_attn_prompt_kernel, seq=seq),
        grid=(n, N_PAIRS),
        in_specs=[blk] * 4,
        out_specs=blk,
        out_shape=jax.ShapeDtypeStruct((n, seq, D_B), BF16),
        scratch_shapes=[pltpu.VMEM((ncfg, nblk, 2 * SPAN, LANES), BF16),
                        pltpu.VMEM((ncfg, nblk, SPAN, LANES), BF16),
                        pltpu.VMEM((ncfg, nblk, SPAN, LANES), BF16),
                        pltpu.VMEM((ncfg, seq, LANES), F32),
                        pltpu.VMEM((ncfg, seq, LANES), F32),
                        pltpu.VMEM((ncfg, seq, LANES), F32)],
        compiler_params=_params("arbitrary", "arbitrary"),
        name="attn_prompt",
    )(q, k, v, gb)


def _attn_sample_kernel(q_ref, kn_ref, vn_ref, kt_ref, vt_ref, gb_ref, out_ref, *, t_new, n_past, heads):
    nq = SAMPLE_PAD
    width = heads * HEAD_DIM_B
    scale = HEAD_DIM_B ** -0.5
    q_all = q_ref[0] * scale
    k_new = kn_ref[0]
    v_new = vn_ref[0]

    lane = lax.broadcasted_iota(jnp.int32, (nq, width), 1)
    qm = jnp.concatenate([jnp.where(lane // HEAD_DIM_B == h, q_all, 0.0) for h in range(heads)], axis=0)
    s_new = _qk(qm.astype(BF16), k_new.astype(BF16))
    t_row = lax.broadcasted_iota(jnp.int32, (nq, nq), 0)
    t_col = lax.broadcasted_iota(jnp.int32, (nq, nq), 1)

    dist = (n_past + lax.broadcasted_iota(jnp.int32, (nq, n_past), 0)
            - lax.broadcasted_iota(jnp.int32, (nq, n_past), 1))
    far = 1 << 30
    masks, starts = [], []
    for window, dil in DSWA_CONFIGS:
        start = n_past - window
        d = dist[:, start:]
        masks.append(jnp.where((d & (dil - 1)) == 0, d, far) <= window)
        starts.append(start)

    new_ok = [jnp.where(((t_row - t_col) & (dil - 1)) == 0, t_row - t_col, -1) >= 0 for _, dil in DSWA_CONFIGS]
    ncfg = len(DSWA_CONFIGS)
    cols = [slice(h * HEAD_DIM_B, (h + 1) * HEAD_DIM_B) for h in range(heads)]

    s_buf = [jnp.dot(q_all[:, cols[h]].astype(BF16), kt_ref[0, h].astype(BF16), preferred_element_type=F32)
             for h in range(heads)]
    stats = []
    for h in range(heads):
        sn = s_new[h * nq:(h + 1) * nq, :]
        per_cfg = []
        for c in range(ncfg):
            sc = jnp.where(masks[c], s_buf[h][:, starts[c]:], NEG)
            snm = jnp.where(new_ok[c], sn, NEG)
            m = jnp.maximum(jnp.max(sc, axis=-1, keepdims=True), jnp.max(snm, axis=-1, keepdims=True))
            p = jnp.exp(sc - m)
            pn = jnp.exp(snm - m)
            l = jnp.sum(p, axis=-1, keepdims=True) + jnp.sum(pn, axis=-1, keepdims=True)
            per_cfg.append((p.astype(BF16), pn, l, m + jnp.log(l)))
        stats.append(per_cfg)
    outs = []
    for h in range(heads):
        vt = vt_ref[0, h].astype(BF16)
        vnh = v_new[:, cols[h]]
        acc = []
        for c in range(ncfg):
            p, pn, l, _ = stats[h][c]
            o = _qk(p, vt[:, starts[c]:])
            for tn in range(t_new):
                o = o + pn[:, tn:tn + 1] * vnh[tn:tn + 1, :]
            acc.append(o / l)
        den = [stats[h][c][3] for c in range(ncfg)]
        mx = jnp.maximum(jnp.maximum(den[0], den[1]), den[2])
        e = [jnp.exp(d - mx) for d in den]
        outs.append((e[0] * acc[0] + e[1] * acc[1] + e[2] * acc[2]) / (e[0] + e[1] + e[2]))
    merged = jnp.concatenate(outs, axis=1)
    out_ref[0] = (merged * gb_ref[0].astype(F32)).astype(BF16)


def _attn_sample(q, k_new, v_new, kt_cache, vt_cache, gb, t_new, heads=8):
    n, _, _, n_past = kt_cache.shape
    assert n_past >= DSWA_CONFIGS[2][0] and t_new <= min(d for _, d in DSWA_CONFIGS[1:]) and t_new <= SAMPLE_PAD
    width = heads * HEAD_DIM_B
    new_blk = pl.BlockSpec((1, SAMPLE_PAD, width), lambda b, h: (b, 0, h))
    cache_blk = pl.BlockSpec((1, heads, HEAD_DIM_B, n_past), lambda b, h: (b, h, 0, 0))
    return pl.pallas_call(
        functools.partial(_attn_sample_kernel, t_new=t_new, n_past=n_past, heads=heads),
        grid=(n, N_HEADS_B // heads),
        in_specs=[new_blk] * 3 + [cache_blk] * 2 + [new_blk],
        out_specs=new_blk,
        out_shape=jax.ShapeDtypeStruct((n, SAMPLE_PAD, D_B), BF16),
        compiler_params=_params("arbitrary", "arbitrary"),
        name="attn_sample",
    )(q, k_new, v_new, kt_cache, vt_cache, gb)


def _outproj_kernel(*refs, n_in):
    ins, (w_ref, g_ref, x_ref, out_ref) = refs[:n_in], refs[n_in:]
    y = None
    off = 0
    for r in ins:
        width = r.shape[1]
        part = jnp.dot(r[...], w_ref[off:off + width, :], preferred_element_type=F32)
        y = part if y is None else y + part
        off += width
    yn = y * lax.rsqrt(jnp.mean(y * y, axis=-1, keepdims=True) + NORM_EPS) * g_ref[...]
    out_ref[...] = x_ref[...] + yn


def _outproj(parts, w_bf16, post_g, x2d, tm):
    rows = x2d.shape[0]
    row_spec = pl.BlockSpec((tm, D_MODEL), lambda i: (i, 0))
    return pl.pallas_call(
        functools.partial(_outproj_kernel, n_in=len(parts)),
        grid=(rows // tm,),
        in_specs=[pl.BlockSpec((tm, p.shape[1]), lambda i: (i, 0)) for p in parts]
        + [pl.BlockSpec(w_bf16.shape, lambda i: (0, 0), pipeline_mode=pl.Buffered(1)),
           _const_spec((1, D_MODEL)), row_spec],
        out_specs=row_spec,
        out_shape=jax.ShapeDtypeStruct((rows, D_MODEL), F32),
        compiler_params=_params("arbitrary"),
        name="outproj",
    )(*parts, w_bf16, post_g[None, :], x2d)


def _odd_inproj_kernel(x_ref, g_ref, w_ref, lbp_ref, q_ref, lf_ref, k_ref, v_ref, gt_ref, *, layer):
    x = x_ref[...]
    ms = jnp.mean(x * x, axis=-1, keepdims=True)
    h = (x * lax.rsqrt(ms + NORM_EPS) * g_ref[...]).astype(BF16)

    def seg(j):
        return jnp.dot(h, w_ref[:, j * D_C:(j + 1) * D_C], preferred_element_type=F32)

    lbp = lbp_ref[...]
    e = jnp.exp(lbp - jnp.max(lbp, axis=0, keepdims=True))
    sm = e / jnp.sum(e, axis=0, keepdims=True)
    lb = jnp.sum(sm[1:layer + 1], axis=0, keepdims=True)

    q_ref[...] = _silu(seg(0)).astype(BF16)
    f = lb + (1.0 - lb) * jax.nn.sigmoid(seg(1))
    lf_ref[...] = jnp.log(f)
    k_ref[...] = (1.0 - f).astype(BF16)
    v_ref[...] = seg(2).astype(BF16)
    gt_ref[...] = _silu(seg(3)).astype(BF16)


def _odd_inproj(x2d, pre_g, w_bf16, hgrn_lb, layer, tm):
    rows = x2d.shape[0]
    row_spec = pl.BlockSpec((tm, D_MODEL), lambda i: (i, 0))
    wide = pl.BlockSpec((tm, D_C), lambda i: (i, 0))
    bf = jax.ShapeDtypeStruct((rows, D_C), BF16)
    return pl.pallas_call(
        functools.partial(_odd_inproj_kernel, layer=layer),
        grid=(rows // tm,),
        in_specs=[row_spec, _const_spec((1, D_MODEL)),
                  pl.BlockSpec(w_bf16.shape, lambda i: (0, 0), pipeline_mode=pl.Buffered(1)),
                  _const_spec(hgrn_lb.shape)],
        out_specs=[wide] * 5,
        out_shape=(bf, jax.ShapeDtypeStruct((rows, D_C), F32), bf, bf, bf),
        compiler_params=_params("arbitrary"),
        name="odd_inproj",
    )(x2d, pre_g[None, :], w_bf16, hgrn_lb)


DIAG = 4
SHIFT_PAD = 8


def _hgrn_kernel(q_ref, lf_ref, k_ref, v_ref, gt_ref, gn_ref, s0_ref,
                 o_ref, sfin_ref, st_s, bp_s, kp_s, vp_s, *, tile, t_valid, nt, hb):
    t = pl.program_id(2)

    @pl.when(t == 0)
    def _():
        for h in range(hb):
            st_s[h] = s0_ref[0, h].T

    row = lax.broadcasted_iota(jnp.int32, (tile, LANES), 0)
    rr = lax.broadcasted_iota(jnp.int32, (tile, tile), 0)
    cc = lax.broadcasted_iota(jnp.int32, (tile, tile), 1)
    tri = (rr >= cc).astype(BF16)
    split_bit = 31 - lax.clz(jnp.where(rr > cc, rr ^ cc, 0))
    zpad = jnp.zeros((SHIFT_PAD, LANES), F32)
    same_block = [(row % DIAG) >= j for j in range(DIAG)]
    levels = []
    m = DIAG
    while 2 * m <= tile:
        levels.append(m)
        m *= 2

    heads = range(hb)
    cols = [slice(h * LANES, (h + 1) * LANES) for h in heads]
    qf, kf, vb, pieces = [], [], [], []
    for h in heads:
        lf = lf_ref[0, :, cols[h]] * LOG2_E
        k_h = k_ref[0, :, cols[h]].astype(F32)
        if t_valid < tile:
            lf = jnp.where(row < t_valid, lf, 0.0)
            k_h = jnp.where(row < t_valid, k_h, 0.0)
        kf.append(k_h)
        qf.append(q_ref[0, :, cols[h]].astype(F32))
        vb.append(v_ref[0, :, cols[h]])
        p1 = lf.astype(BF16)
        r1 = lf - p1.astype(F32)
        p2 = r1.astype(BF16)
        pieces.append((p1, p2, (r1 - p2.astype(F32)).astype(BF16)))

    b = []
    for h in heads:
        b_h = (jnp.dot(tri, pieces[h][0], preferred_element_type=F32)
               + jnp.dot(tri, pieces[h][1], preferred_element_type=F32)
               + jnp.dot(tri, pieces[h][2], preferred_element_type=F32))
        b.append(b_h)
        bp_s[h, 0:SHIFT_PAD, :] = zpad
        kp_s[h, 0:SHIFT_PAD, :] = zpad
        vp_s[h, 0:SHIFT_PAD, :] = zpad
        bp_s[h, SHIFT_PAD:SHIFT_PAD + tile, :] = b_h
        kp_s[h, SHIFT_PAD:SHIFT_PAD + tile, :] = kf[h]
        vp_s[h, SHIFT_PAD:SHIFT_PAD + tile, :] = vb[h].astype(F32)

    attn = [jnp.zeros((tile, tile), F32) for _ in heads]
    for m in levels:
        nb = tile // (2 * m)
        for h in heads:
            if nb > 1:
                mid = bp_s[h, pl.ds(SHIFT_PAD + m - 1, nb, stride=2 * m), :]
                mid = jnp.broadcast_to(mid[:, None, :], (nb, 2 * m, LANES)).reshape(tile, LANES)
            else:
                mid = jnp.broadcast_to(bp_s[h, SHIFT_PAD + m - 1:SHIFT_PAD + m, :], (tile, LANES))
            decay = jnp.exp2(-jnp.abs(b[h] - mid))
            attn[h] = jnp.where(split_bit == m.bit_length() - 1,
                                _qk((qf[h] * decay).astype(BF16), (kf[h] * decay).astype(BF16)), attn[h])
    o = [jnp.dot(attn[h].astype(BF16), vb[h], preferred_element_type=F32) for h in heads]

    for h in heads:
        vf = vp_s[h, SHIFT_PAD:SHIFT_PAD + tile, :]
        o[h] = o[h] + jnp.sum(qf[h] * kf[h], axis=-1, keepdims=True) * vf
        for j in range(1, DIAG):
            bj = bp_s[h, SHIFT_PAD - j:SHIFT_PAD - j + tile, :]
            kj = kp_s[h, SHIFT_PAD - j:SHIFT_PAD - j + tile, :]
            vj = vp_s[h, SHIFT_PAD - j:SHIFT_PAD - j + tile, :]
            e = jnp.where(same_block[j], b[h] - bj, NEG)
            w = jnp.sum(qf[h] * kj * jnp.exp2(e), axis=-1, keepdims=True)
            o[h] = o[h] + w * vj

    st = [st_s[h] for h in heads]
    for h in heads:
        o[h] = o[h] + _qk((qf[h] * jnp.exp2(b[h])).astype(BF16), st[h].astype(BF16))
    for h in heads:
        b_last = bp_s[h, SHIFT_PAD + tile - 1:SHIFT_PAD + tile, :]
        kdec = (kf[h] * jnp.exp2(b_last - b[h])).astype(BF16)
        st_s[h] = jnp.exp2(b_last) * st[h] + lax.dot_general(
            vb[h], kdec, (((0,), (0,)), ((), ())), preferred_element_type=F32)

    for h in heads:
        on = o[h] * lax.rsqrt(jnp.mean(o[h] * o[h], axis=-1, keepdims=True) + NORM_EPS) * gn_ref[...]
        o_ref[0, :, cols[h]] = (on * gt_ref[0, :, cols[h]].astype(F32)).astype(BF16)

    @pl.when(t == nt - 1)
    def _():
        for h in range(hb):
            sfin_ref[0, h] = st_s[h].T


def _hgrn(q, lf, k, v, gt, g_norm, s0, tile, t_valid, hb):
    n, t, _ = q.shape
    nt = t // tile
    blk = pl.BlockSpec((1, tile, hb * LANES), lambda b, h, i: (b, i, h))
    st_blk = pl.BlockSpec((1, hb, HGRN_HEAD, HGRN_HEAD), lambda b, h, i: (b, h, 0, 0))
    pad = pltpu.VMEM((hb, SHIFT_PAD + tile, LANES), F32)
    return pl.pallas_call(
        functools.partial(_hgrn_kernel, tile=tile, t_valid=t_valid, nt=nt, hb=hb),
        grid=(n, N_HEADS_C // hb, nt),
        in_specs=[blk] * 5 + [_const_spec((1, LANES)), st_blk],
        out_specs=[blk, st_blk],
        out_shape=(jax.ShapeDtypeStruct((n, t, D_C), BF16),
                   jax.ShapeDtypeStruct((n, N_HEADS_C, HGRN_HEAD, HGRN_HEAD), F32)),
        scratch_shapes=[pltpu.VMEM((hb, HGRN_HEAD, HGRN_HEAD), F32),
                        pad, pad, pad],
        compiler_params=_params("arbitrary", "arbitrary", "arbitrary"),
        name="hgrn",
    )(q, lf, k, v, gt, g_norm[None, :], s0)


def _tiles(n, t):
    rows = n * t
    tm = min(256, rows)
    tm_wide = min(512, rows)
    tc = min(256, t)
    th = min(128, t)
    hb = 8 if t // th > 1 else N_HEADS_C
    return tm, tm_wide, tc, th, hb


def _trunk(x, pos, conv_hist, kv_cache, state0, t_valid, prm):
    n, t, _ = x.shape
    rows = n * t
    tm, tm_wide, tc, th, hb = _tiles(n, t)
    x2d = x.reshape(rows, D_MODEL)
    cos_t, sin_t = _rope_tables(pos)
    if cos_t.shape[0] < tm:
        cos_t = jnp.tile(cos_t, (tm // t, 1))
        sin_t = jnp.tile(sin_t, (tm // t, 1))
    kv_t = None
    if kv_cache is None:
        u, ga, q, k, v, gb, *kv_t = _even_inproj(x2d, prm["pre0"], prm["w_in_ab"], cos_t, sin_t, tm, seq_t=t)
    else:
        u, ga, q, k, v, gb = _even_inproj(x2d, prm["pre0"], prm["w_in_ab"], cos_t, sin_t, tm)
    shp = (n, t, D_A)
    ya, hist = _conv_module(u.reshape(shp), ga.reshape(shp), conv_hist, prm["conv_w"], prm["conv_b"],
                            prm["ln_g"], prm["ln_b"], tc, t_valid)
    if kv_cache is None:
        yb = _attn_prompt(q.reshape(shp), k.reshape(shp), v.reshape(shp), gb.reshape(shp))
    else:
        yb = _attn_sample(q.reshape(shp), k.reshape(shp), v.reshape(shp), kv_cache[0], kv_cache[1],
                          gb.reshape(shp), t_valid)
    x1 = _outproj([ya.reshape(rows, D_A), yb.reshape(rows, D_B)], prm["w_out_ab"], prm["post0"], x2d, tm_wide)
    qc, lf, kc, vc, gt = _odd_inproj(x1, prm["pre1"], prm["w_in_c"], prm["hgrn_lb"], 1, tm)
    wide = (n, t, D_C)
    oc, s_fin = _hgrn(qc.reshape(wide), lf.reshape(wide), kc.reshape(wide), vc.reshape(wide),
                      gt.reshape(wide), prm["gnorm"], state0, th, t_valid, hb)
    x2 = _outproj([oc.reshape(rows, D_C)], prm["w_out_c"], prm["post1"], x1, tm_wide)
    if kv_t is None:
        kv_t = (k.reshape(shp), v.reshape(shp))
    return x2.reshape(n, t, D_MODEL), hist, kv_t[0], kv_t[1], s_fin


def kernel(x_prompt, x_sample, cache_conv, cache_swa_k, cache_swa_v, state_hgrn, pre_norm, post_norm,
           w_in_ab, w_out_ab, conv_w, conv_b, conv_ln_g, conv_ln_b, w_in_c, w_out_c, hgrn_gnorm, hgrn_lb):
    n_p, seq_p, _ = x_prompt.shape
    n_s, seq_s, _ = x_sample.shape
    hist_rows = CONV_WIDTH - 1
    prm = dict(pre0=pre_norm[0], pre1=pre_norm[1], post0=post_norm[0], post1=post_norm[1],
               w_in_ab=w_in_ab[0].astype(BF16), w_out_ab=w_out_ab[0].astype(BF16),
               conv_w=conv_w[0], conv_b=conv_b[0], ln_g=conv_ln_g[0], ln_b=conv_ln_b[0],
               w_in_c=w_in_c[0].astype(BF16), w_out_c=w_out_c[0].astype(BF16),
               gnorm=hgrn_gnorm[0], hgrn_lb=hgrn_lb)

    pos_p = jnp.arange(seq_p, dtype=jnp.int32)
    yp, hist_p, k_p, v_p, s_p = _trunk(
        x_prompt, pos_p, jnp.zeros((n_p, HALO, D_A), F32), None,
        jnp.zeros((n_p, N_HEADS_C, HGRN_HEAD, HGRN_HEAD), F32), seq_p, prm)

    pad = SAMPLE_PAD - seq_s
    xs = jnp.pad(x_sample, ((0, 0), (0, pad), (0, 0)))
    pos_s = PAST_LEN + jnp.arange(SAMPLE_PAD, dtype=jnp.int32)
    hist_s = jnp.pad(cache_conv[0], ((0, 0), (HALO - hist_rows, 0), (0, 0)))
    kv = (jnp.transpose(cache_swa_k[0], (0, 2, 3, 1)), jnp.transpose(cache_swa_v[0], (0, 2, 3, 1)))
    ys, hist_s, k_s, v_s, s_s = _trunk(
        xs, pos_s, hist_s, kv, state_hgrn[0], seq_s, prm)

    heads = (N_HEADS_B, HEAD_DIM_B)
    return (yp, ys[:, :seq_s],
            hist_p[None, :, HALO - hist_rows:], hist_s[None, :, HALO - hist_rows:],
            jnp.transpose(k_p, (0, 3, 1, 2))[None], jnp.transpose(v_p, (0, 3, 1, 2))[None],
            k_s[:, :seq_s].reshape(1, n_s, seq_s, *heads), v_s[:, :seq_s].reshape(1, n_s, seq_s, *heads),
            s_p[None], s_s[None])
```

```python
import functools

import jax
import jax.numpy as jnp
from jax import lax
from jax.experimental import pallas as pl
from jax.experimental.pallas import tpu as pltpu

F32 = jnp.float32
BF16 = jnp.bfloat16

D_MODEL = 1024
D_A = 1024
D_B = 1024
CONV_WIDTH = 31
HALO = 32
HEAD_DIM_B = 64
N_HEADS_B = 16
LANES = 128
SUBLANES = 8
N_PAIRS = D_B // LANES
DSWA_CONFIGS = ((128, 1), (512, 4), (2048, 16))
SPAN = 128
ROPE_THETA = 10000.0
PAST_LEN = 16384
HGRN_HEAD = 128
N_HEADS_C = 16
D_C = N_HEADS_C * HGRN_HEAD
NORM_EPS = 1e-6
NEG = -1e30
LOG2_E = 1.4426950408889634
SAMPLE_PAD = 16
ATTN_GROUP = 4
VMEM_LIMIT = 56 * 1024 * 1024


def _silu(x):
    return x * jax.nn.sigmoid(x)


def _params(*sem):
    return pltpu.CompilerParams(dimension_semantics=sem, vmem_limit_bytes=VMEM_LIMIT)


def _const_spec(shape):
    return pl.BlockSpec(shape, lambda *_: (0,) * len(shape))


def _rope_table_kernel(pos_ref, invf_ref, sign_ref, cos_ref, sin_ref):
    ang = pos_ref[...] * invf_ref[...]
    cos_ref[...] = jnp.cos(ang)
    sin_ref[...] = jnp.sin(ang) * sign_ref[...]


def _rope_tables(pos):
    t = pos.shape[0]
    half = HEAD_DIM_B // 2
    inv_freq = ROPE_THETA ** (-jnp.arange(half, dtype=F32) / half)
    invf = jnp.tile(inv_freq, LANES // half)[None, :]
    lane = jnp.arange(LANES)
    sign = jnp.where(lane % HEAD_DIM_B < half, -1.0, 1.0).astype(F32)[None, :]
    posb = jnp.broadcast_to(pos.astype(F32)[:, None], (t, LANES))
    return pl.pallas_call(
        _rope_table_kernel,
        out_shape=(jax.ShapeDtypeStruct((t, LANES), F32),) * 2,
        name="rope_tables",
    )(posb, invf, sign)


def _even_inproj_kernel(x_ref, g_ref, w_ref, cos_ref, sin_ref,
                        u_ref, ga_ref, q_ref, k_ref, v_ref, gb_ref, *kv_t_refs):
    x = x_ref[...]
    ms = jnp.mean(x * x, axis=-1, keepdims=True)
    h = (x * lax.rsqrt(ms + NORM_EPS) * g_ref[...]).astype(BF16)

    def seg(j):
        return jnp.dot(h, w_ref[:, j * D_A:(j + 1) * D_A], preferred_element_type=F32)

    u_ref[...] = seg(0) * jax.nn.sigmoid(seg(1))
    ga_ref[...] = _silu(seg(2)).astype(BF16)

    cos = cos_ref[...]
    sin = sin_ref[...]
    lane = lax.broadcasted_iota(jnp.int32, cos.shape, 1)
    first_half = (lane % HEAD_DIM_B) < (HEAD_DIM_B // 2)

    def emit(val, out_ref, rotate, t_ref):
        for c in range(N_PAIRS):
            xc = val[:, c * LANES:(c + 1) * LANES]
            if rotate:
                partner = jnp.where(first_half,
                                    pltpu.roll(xc, LANES - HEAD_DIM_B // 2, 1),
                                    pltpu.roll(xc, HEAD_DIM_B // 2, 1))
                xc = xc * cos + partner * sin
            out_ref[:, c * LANES:(c + 1) * LANES] = xc
            if t_ref is not None:
                xt = xc.T
                t_ref[0, 2 * c] = xt[0:HEAD_DIM_B]
                t_ref[0, 2 * c + 1] = xt[HEAD_DIM_B:LANES]

    kt_ref, vt_ref = kv_t_refs if kv_t_refs else (None, None)
    emit(seg(3), q_ref, True, None)
    emit(seg(4), k_ref, True, kt_ref)
    emit(seg(5), v_ref, False, vt_ref)
    gb_ref[...] = _silu(seg(6)).astype(BF16)


def _even_inproj(x2d, pre_g, w_bf16, cos_t, sin_t, tm, seq_t=None):
    rows = x2d.shape[0]
    nt_tab = cos_t.shape[0] // tm
    row_spec = pl.BlockSpec((tm, D_MODEL), lambda i: (i, 0))
    tab_spec = pl.BlockSpec((tm, LANES), lambda i: (i % nt_tab, 0))
    f32_out = jax.ShapeDtypeStruct((rows, D_A), F32)
    bf_out = jax.ShapeDtypeStruct((rows, D_A), BF16)
    out_specs = [row_spec] * 6
    out_shape = (f32_out, bf_out, f32_out, f32_out, f32_out, bf_out)
    if seq_t is not None:
        nt = seq_t // tm
        t_spec = pl.BlockSpec((1, N_HEADS_B, HEAD_DIM_B, tm), lambda i: (i // nt, 0, 0, i % nt))
        t_out = jax.ShapeDtypeStruct((rows // seq_t, N_HEADS_B, HEAD_DIM_B, seq_t), F32)
        out_specs = out_specs + [t_spec] * 2
        out_shape = out_shape + (t_out,) * 2
    return pl.pallas_call(
        _even_inproj_kernel,
        grid=(rows // tm,),
        in_specs=[row_spec, _const_spec((1, D_MODEL)),
                  pl.BlockSpec(w_bf16.shape, lambda i: (0, 0), pipeline_mode=pl.Buffered(1)),
                  tab_spec, tab_spec],
        out_specs=out_specs,
        out_shape=out_shape,
        compiler_params=_params("arbitrary"),
        name="even_inproj",
    )(x2d, pre_g[None, :], w_bf16, cos_t, sin_t)


def _conv_kernel(u_ref, ga_ref, buf_ref, w_ref, cb_ref, lg_ref, lb_ref,
                 ya_ref, tail_ref, ext_ref, y_ref, part_ref, *, tc, t_valid, nt):
    t = pl.program_id(1)

    @pl.when(t == 0)
    def _():
        ext_ref[0:HALO, :] = buf_ref[0]
        ext_ref[HALO + tc:HALO + tc + SUBLANES, :] = jnp.zeros((SUBLANES, D_A), F32)

    ext_ref[HALO:HALO + tc, :] = u_ref[0]
    first = HALO - (CONV_WIDTH - 1)
    for c in range(D_A // LANES):
        cols = slice(c * LANES, (c + 1) * LANES)
        acc = jnp.zeros((tc, LANES), F32)
        for rho in range(SUBLANES):
            part = None
            for k in range(CONV_WIDTH):
                if (first + k) % SUBLANES == rho:
                    base = first + k - rho
                    term = ext_ref[base:base + tc + SUBLANES, cols] * w_ref[k:k + 1, cols]
                    part = term if part is None else part + term
            if part is None:
                continue
            if rho == 0:
                acc = acc + part[0:tc]
            else:
                part_ref[rho] = part
                acc = acc + part_ref[rho, rho:rho + tc, :]
        y_ref[:, cols] = acc + cb_ref[:, cols]
    y = y_ref[...]
    yc = y - jnp.mean(y, axis=-1, keepdims=True)
    yn = yc * lax.rsqrt(jnp.mean(yc * yc, axis=-1, keepdims=True) + NORM_EPS)
    yn = yn * lg_ref[...] + lb_ref[...]
    ya_ref[0] = (_silu(yn) * ga_ref[0].astype(F32)).astype(BF16)

    @pl.when(t == nt - 1)
    def _():
        tail_ref[0] = ext_ref[t_valid:t_valid + HALO, :]

    if nt > 1:
        ext_ref[0:HALO, :] = ext_ref[tc:tc + HALO, :]


def _conv_module(u, ga, buf, conv_w, conv_b, ln_g, ln_b, tc, t_valid):
    n, t, _ = u.shape
    nt = t // tc
    blk = pl.BlockSpec((1, tc, D_A), lambda b, i: (b, i, 0))
    halo = pl.BlockSpec((1, HALO, D_A), lambda b, i: (b, 0, 0))
    last_valid = t_valid - (nt - 1) * tc
    return pl.pallas_call(
        functools.partial(_conv_kernel, tc=tc, t_valid=last_valid, nt=nt),
        grid=(n, nt),
        in_specs=[blk, blk, halo, _const_spec((CONV_WIDTH, D_A)),
                  _const_spec((1, D_A)), _const_spec((1, D_A)), _const_spec((1, D_A))],
        out_specs=[blk, halo],
        out_shape=(jax.ShapeDtypeStruct((n, t, D_A), BF16),
                   jax.ShapeDtypeStruct((n, HALO, D_A), F32)),
        scratch_shapes=[pltpu.VMEM((HALO + tc + SUBLANES, D_A), F32), pltpu.VMEM((tc, D_A), F32),
                        pltpu.VMEM((SUBLANES, tc + SUBLANES, LANES), F32)],
        compiler_params=_params("arbitrary", "arbitrary"),
        name="conv_module",
    )(u, ga, buf, conv_w, conv_b[None, :], ln_g[None, :], ln_b[None, :])


def _qk(q, k):
    return lax.dot_general(q, k, (((1,), (1,)), ((), ())), preferred_element_type=F32)


def _attn_prompt_kernel(q_ref, k_ref, v_ref, gb_ref, out_ref, q_s, k_s, v_s, o_s, m_s, l_s, *, seq):
    nblk = seq // SPAN
    scale = HEAD_DIM_B ** -0.5 * LOG2_E
    lo = lax.broadcasted_iota(jnp.int32, (SPAN, LANES), 1) < HEAD_DIM_B

    def rows(c, blk):
        dil = DSWA_CONFIGS[c][1]
        per_seq = nblk // dil
        return dil * SPAN * (blk % per_seq) + blk // per_seq, dil

    def take(ref, start, stride):
        if stride == 1:
            return ref[0, start:start + SPAN, :]
        return ref[0, pl.ds(start, SPAN, stride=stride), :]

    for c in range(len(DSWA_CONFIGS)):
        for blk in range(nblk):
            start, stride = rows(c, blk)
            qb = take(q_ref, start, stride) * scale
            q_s[c, blk, 0:SPAN, :] = jnp.where(lo, qb, 0.0).astype(BF16)
            q_s[c, blk, SPAN:2 * SPAN, :] = jnp.where(lo, 0.0, qb).astype(BF16)
            k_s[c, blk] = take(k_ref, start, stride).astype(BF16)
            vb = take(v_ref, start, stride)
            v_s[c, blk, 0] = jnp.where(lo, vb, 1.0).astype(BF16)
            v_s[c, blk, 1] = jnp.where(lo, 1.0, vb).astype(BF16)

    row = lax.broadcasted_iota(jnp.int32, (2 * SPAN, SPAN), 0) % SPAN
    col = lax.broadcasted_iota(jnp.int32, (2 * SPAN, SPAN), 1)
    upper = col >= row
    diag = col == row
    lower = col < row

    def has_prev(c, blk):
        return blk % (nblk // DSWA_CONFIGS[c][1]) > 0

    def scores(c, blk):
        q2 = q_s[c, blk]
        s_own = _qk(q2, k_s[c, blk])
        s_far = _qk(q2, k_s[c, blk - 1]) if has_prev(c, blk) else NEG
        return s_own, s_far

    def probabilities(s_own, s_far):
        s_self = jnp.sum(jnp.where(diag, s_own, 0.0), axis=-1, keepdims=True)
        s = jnp.where(upper, s_far, s_own)
        m = jnp.maximum(jnp.max(s, axis=-1, keepdims=True), s_self)
        p = jnp.exp2(s - m)
        p_self = jnp.exp2(s_self - m)
        p_own = jnp.where(lower, p, jnp.where(diag, p_self, 0.0)).astype(BF16)
        p_far = jnp.where(upper, p, 0.0).astype(BF16)
        return p_own, p_far, m

    def finish(c, blk, p_own, p_far, m):
        o = []
        for h in range(2):
            ph = slice(h * SPAN, (h + 1) * SPAN)
            o_h = jnp.dot(p_own[ph], v_s[c, blk, h], preferred_element_type=F32)
            if has_prev(c, blk):
                o_h = o_h + jnp.dot(p_far[ph], v_s[c, blk - 1, h], preferred_element_type=F32)
            o.append(o_h)
        m = jnp.broadcast_to(m, (2 * SPAN, LANES))
        start, stride = rows(c, blk)
        dst = pl.ds(start, SPAN, stride=stride) if stride > 1 else pl.ds(start, SPAN)
        o_s[c, dst, :] = jnp.where(lo, o[0], o[1])
        m_s[c, dst, :] = jnp.where(lo, m[0:SPAN], m[SPAN:2 * SPAN])
        l_s[c, dst, :] = jnp.where(lo, o[1], o[0])

    todo = [(c, blk) for c in range(len(DSWA_CONFIGS)) for blk in range(nblk)]
    groups = [todo[i:i + ATTN_GROUP] for i in range(0, len(todo), ATTN_GROUP)]
    pending = [scores(*cb) for cb in groups[0]]
    for gi, group in enumerate(groups):
        ahead = [scores(*cb) for cb in groups[gi + 1]] if gi + 1 < len(groups) else []
        probs = [probabilities(*sc) for sc in pending]
        for cb, pr in zip(group, probs):
            finish(*cb, *pr)
        pending = ahead

    ma, mb, mc = m_s[0], m_s[1], m_s[2]
    mx = jnp.maximum(jnp.maximum(ma, mb), mc)
    ea, eb, ec = jnp.exp2(ma - mx), jnp.exp2(mb - mx), jnp.exp2(mc - mx)
    la, lb, lc = (pltpu.roll(l_s[c], HEAD_DIM_B, 1) for c in range(3))
    merged = (ea * o_s[0] + eb * o_s[1] + ec * o_s[2]) / (ea * la + eb * lb + ec * lc)
    out_ref[0] = (merged * gb_ref[0].astype(F32)).astype(BF16)


def _attn_prompt(q, k, v, gb):
    n, seq, _ = q.shape
    nblk = seq // SPAN
    assert all(w // d == SPAN and nblk % d == 0 for w, d in DSWA_CONFIGS)
    ncfg = len(DSWA_CONFIGS)
    blk = pl.BlockSpec((1, seq, LANES), lambda b, h: (b, 0, h))
    return pl.pallas_call(
        functools.partial(_attn_prompt_kernel, seq=seq),
        grid=(n, N_PAIRS),
        in_specs=[blk] * 4,
        out_specs=blk,
        out_shape=jax.ShapeDtypeStruct((n, seq, D_B), BF16),
        scratch_shapes=[pltpu.VMEM((ncfg, nblk, 2 * SPAN, LANES), BF16),
                        pltpu.VMEM((ncfg, nblk, SPAN, LANES), BF16),
                        pltpu.VMEM((ncfg, nblk, 2, SPAN, LANES), BF16),
                        pltpu.VMEM((ncfg, seq, LANES), F32),
                        pltpu.VMEM((ncfg, seq, LANES), F32),
                        pltpu.VMEM((ncfg, seq, LANES), F32)],
        compiler_params=_params("arbitrary", "arbitrary"),
        name="attn_prompt",
    )(q, k, v, gb)


def _attn_sample_kernel(q_ref, kn_ref, vn_ref, kt_ref, vt_ref, gb_ref, out_ref, *, t_new, n_past, heads):
    nq = SAMPLE_PAD
    width = heads * HEAD_DIM_B
    scale = HEAD_DIM_B ** -0.5
    q_all = q_ref[0] * scale
    k_new = kn_ref[0]
    v_new = vn_ref[0]

    lane = lax.broadcasted_iota(jnp.int32, (nq, width), 1)
    qm = jnp.concatenate([jnp.where(lane // HEAD_DIM_B == h, q_all, 0.0) for h in range(heads)], axis=0)
    s_new = _qk(qm.astype(BF16), k_new.astype(BF16))
    t_row = lax.broadcasted_iota(jnp.int32, (nq, nq), 0)
    t_col = lax.broadcasted_iota(jnp.int32, (nq, nq), 1)

    dist = (n_past + lax.broadcasted_iota(jnp.int32, (nq, n_past), 0)
            - lax.broadcasted_iota(jnp.int32, (nq, n_past), 1))
    far = 1 << 30
    masks, starts = [], []
    for window, dil in DSWA_CONFIGS:
        start = n_past - window
        d = dist[:, start:]
        masks.append(jnp.where((d & (dil - 1)) == 0, d, far) <= window)
        starts.append(start)

    new_ok = [jnp.where(((t_row - t_col) & (dil - 1)) == 0, t_row - t_col, -1) >= 0 for _, dil in DSWA_CONFIGS]
    ncfg = len(DSWA_CONFIGS)
    cols = [slice(h * HEAD_DIM_B, (h + 1) * HEAD_DIM_B) for h in range(heads)]

    s_buf = [jnp.dot(q_all[:, cols[h]].astype(BF16), kt_ref[0, h].astype(BF16), preferred_element_type=F32)
             for h in range(heads)]
    stats = []
    for h in range(heads):
        sn = s_new[h * nq:(h + 1) * nq, :]
        per_cfg = []
        for c in range(ncfg):
            sc = jnp.where(masks[c], s_buf[h][:, starts[c]:], NEG)
            snm = jnp.where(new_ok[c], sn, NEG)
            m = jnp.maximum(jnp.max(sc, axis=-1, keepdims=True), jnp.max(snm, axis=-1, keepdims=True))
            p = jnp.exp(sc - m)
            pn = jnp.exp(snm - m)
            l = jnp.sum(p, axis=-1, keepdims=True) + jnp.sum(pn, axis=-1, keepdims=True)
            per_cfg.append((p.astype(BF16), pn, l, m + jnp.log(l)))
        stats.append(per_cfg)
    outs = []
    for h in range(heads):
        vt = vt_ref[0, h].astype(BF16)
        vnh = v_new[:, cols[h]]
        acc = []
        for c in range(ncfg):
            p, pn, l, _ = stats[h][c]
            o = _qk(p, vt[:, starts[c]:])
            for tn in range(t_new):
                o = o + pn[:, tn:tn + 1] * vnh[tn:tn + 1, :]
            acc.append(o / l)
        den = [stats[h][c][3] for c in range(ncfg)]
        mx = jnp.maximum(jnp.maximum(den[0], den[1]), den[2])
        e = [jnp.exp(d - mx) for d in den]
        outs.append((e[0] * acc[0] + e[1] * acc[1] + e[2] * acc[2]) / (e[0] + e[1] + e[2]))
    merged = jnp.concatenate(outs, axis=1)
    out_ref[0] = (merged * gb_ref[0].astype(F32)).astype(BF16)


def _attn_sample(q, k_new, v_new, kt_cache, vt_cache, gb, t_new, heads=8):
    n, _, _, n_past = kt_cache.shape
    assert n_past >= DSWA_CONFIGS[2][0] and t_new <= min(d for _, d in DSWA_CONFIGS[1:]) and t_new <= SAMPLE_PAD
    width = heads * HEAD_DIM_B
    new_blk = pl.BlockSpec((1, SAMPLE_PAD, width), lambda b, h: (b, 0, h))
    cache_blk = pl.BlockSpec((1, heads, HEAD_DIM_B, n_past), lambda b, h: (b, h, 0, 0))
    return pl.pallas_call(
        functools.partial(_attn_sample_kernel, t_new=t_new, n_past=n_past, heads=heads),
        grid=(n, N_HEADS_B // heads),
        in_specs=[new_blk] * 3 + [cache_blk] * 2 + [new_blk],
        out_specs=new_blk,
        out_shape=jax.ShapeDtypeStruct((n, SAMPLE_PAD, D_B), BF16),
        compiler_params=_params("arbitrary", "arbitrary"),
        name="attn_sample",
    )(q, k_new, v_new, kt_cache, vt_cache, gb)


def _outproj_kernel(*refs, n_in):
    ins, (w_ref, g_ref, x_ref, out_ref) = refs[:n_in], refs[n_in:]
    y = None
    off = 0
    for r in ins:
        width = r.shape[1]
        part = jnp.dot(r[...], w_ref[off:off + width, :], preferred_element_type=F32)
        y = part if y is None else y + part
        off += width
    yn = y * lax.rsqrt(jnp.mean(y * y, axis=-1, keepdims=True) + NORM_EPS) * g_ref[...]
    out_ref[...] = x_ref[...] + yn


def _outproj(parts, w_bf16, post_g, x2d, tm):
    rows = x2d.shape[0]
    row_spec = pl.BlockSpec((tm, D_MODEL), lambda i: (i, 0))
    return pl.pallas_call(
        functools.partial(_outproj_kernel, n_in=len(parts)),
        grid=(rows // tm,),
        in_specs=[pl.BlockSpec((tm, p.shape[1]), lambda i: (i, 0)) for p in parts]
        + [pl.BlockSpec(w_bf16.shape, lambda i: (0, 0), pipeline_mode=pl.Buffered(1)),
           _const_spec((1, D_MODEL)), row_spec],
        out_specs=row_spec,
        out_shape=jax.ShapeDtypeStruct((rows, D_MODEL), F32),
        compiler_params=_params("arbitrary"),
        name="outproj",
    )(*parts, w_bf16, post_g[None, :], x2d)


def _odd_inproj_kernel(x_ref, g_ref, w_ref, lbp_ref, q_ref, lf_ref, k_ref, v_ref, gt_ref, *, layer):
    x = x_ref[...]
    ms = jnp.mean(x * x, axis=-1, keepdims=True)
    h = (x * lax.rsqrt(ms + NORM_EPS) * g_ref[...]).astype(BF16)

    def seg(j):
        return jnp.dot(h, w_ref[:, j * D_C:(j + 1) * D_C], preferred_element_type=F32)

    lbp = lbp_ref[...]
    e = jnp.exp(lbp - jnp.max(lbp, axis=0, keepdims=True))
    sm = e / jnp.sum(e, axis=0, keepdims=True)
    lb = jnp.sum(sm[1:layer + 1], axis=0, keepdims=True)

    q_ref[...] = _silu(seg(0)).astype(BF16)
    f = lb + (1.0 - lb) * jax.nn.sigmoid(seg(1))
    lf_ref[...] = jnp.log(f)
    k_ref[...] = (1.0 - f).astype(BF16)
    v_ref[...] = seg(2).astype(BF16)
    gt_ref[...] = _silu(seg(3)).astype(BF16)


def _odd_inproj(x2d, pre_g, w_bf16, hgrn_lb, layer, tm):
    rows = x2d.shape[0]
    row_spec = pl.BlockSpec((tm, D_MODEL), lambda i: (i, 0))
    wide = pl.BlockSpec((tm, D_C), lambda i: (i, 0))
    bf = jax.ShapeDtypeStruct((rows, D_C), BF16)
    return pl.pallas_call(
        functools.partial(_odd_inproj_kernel, layer=layer),
        grid=(rows // tm,),
        in_specs=[row_spec, _const_spec((1, D_MODEL)),
                  pl.BlockSpec(w_bf16.shape, lambda i: (0, 0), pipeline_mode=pl.Buffered(1)),
                  _const_spec(hgrn_lb.shape)],
        out_specs=[wide] * 5,
        out_shape=(bf, jax.ShapeDtypeStruct((rows, D_C), F32), bf, bf, bf),
        compiler_params=_params("arbitrary"),
        name="odd_inproj",
    )(x2d, pre_g[None, :], w_bf16, hgrn_lb)


DIAG = 4
SHIFT_PAD = 8


def _hgrn_kernel(q_ref, lf_ref, k_ref, v_ref, gt_ref, gn_ref, s0_ref,
                 o_ref, sfin_ref, st_s, bp_s, kp_s, vp_s, tri_s, split_s, *, tile, t_valid, nt, hb):
    t = pl.program_id(2)

    @pl.when(t == 0)
    def _():
        for h in range(hb):
            st_s[h] = s0_ref[0, h].T

    @pl.when((pl.program_id(0) == 0) & (pl.program_id(1) == 0) & (t == 0))
    def _():
        rr = lax.broadcasted_iota(jnp.int32, (tile, tile), 0)
        cc = lax.broadcasted_iota(jnp.int32, (tile, tile), 1)
        tri_s[...] = jnp.where(rr >= cc, 1.0, 0.0).astype(BF16)
        split_s[...] = 31 - lax.clz(jnp.where(rr > cc, rr ^ cc, 0))

    row = lax.broadcasted_iota(jnp.int32, (tile, LANES), 0)
    tri = tri_s[...]
    split_bit = split_s[...]
    zpad = jnp.zeros((SHIFT_PAD, LANES), F32)
    same_block = [(row % DIAG) >= j for j in range(DIAG)]
    levels = []
    m = DIAG
    while 2 * m <= tile:
        levels.append(m)
        m *= 2

    heads = range(hb)
    cols = [slice(h * LANES, (h + 1) * LANES) for h in heads]
    qf, kf, vb, pieces = [], [], [], []
    for h in heads:
        lf = lf_ref[0, :, cols[h]] * LOG2_E
        k_h = k_ref[0, :, cols[h]].astype(F32)
        if t_valid < tile:
            lf = jnp.where(row < t_valid, lf, 0.0)
            k_h = jnp.where(row < t_valid, k_h, 0.0)
        kf.append(k_h)
        qf.append(q_ref[0, :, cols[h]].astype(F32))
        vb.append(v_ref[0, :, cols[h]])
        p1 = lf.astype(BF16)
        r1 = lf - p1.astype(F32)
        p2 = r1.astype(BF16)
        pieces.append((p1, p2, (r1 - p2.astype(F32)).astype(BF16)))

    b = []
    for h in heads:
        b_h = (jnp.dot(tri, pieces[h][0], preferred_element_type=F32)
               + jnp.dot(tri, pieces[h][1], preferred_element_type=F32)
               + jnp.dot(tri, pieces[h][2], preferred_element_type=F32))
        b.append(b_h)
        bp_s[h, 0:SHIFT_PAD, :] = zpad
        kp_s[h, 0:SHIFT_PAD, :] = zpad
        vp_s[h, 0:SHIFT_PAD, :] = zpad
        bp_s[h, SHIFT_PAD:SHIFT_PAD + tile, :] = b_h
        kp_s[h, SHIFT_PAD:SHIFT_PAD + tile, :] = kf[h]
        vp_s[h, SHIFT_PAD:SHIFT_PAD + tile, :] = vb[h].astype(F32)

    attn = [jnp.zeros((tile, tile), F32) for _ in heads]
    for m in levels:
        nb = tile // (2 * m)
        for h in heads:
            if nb > 1:
                mid = bp_s[h, pl.ds(SHIFT_PAD + m - 1, nb, stride=2 * m), :]
                mid = jnp.broadcast_to(mid[:, None, :], (nb, 2 * m, LANES)).reshape(tile, LANES)
            else:
                mid = jnp.broadcast_to(bp_s[h, SHIFT_PAD + m - 1:SHIFT_PAD + m, :], (tile, LANES))
            decay = jnp.exp2(-jnp.abs(b[h] - mid))
            attn[h] = jnp.where(split_bit == m.bit_length() - 1,
                                _qk((qf[h] * decay).astype(BF16), (kf[h] * decay).astype(BF16)), attn[h])
    o = [jnp.dot(attn[h].astype(BF16), vb[h], preferred_element_type=F32) for h in heads]

    for h in heads:
        vf = vp_s[h, SHIFT_PAD:SHIFT_PAD + tile, :]
        o[h] = o[h] + jnp.sum(qf[h] * kf[h], axis=-1, keepdims=True) * vf
        for j in range(1, DIAG):
            bj = bp_s[h, SHIFT_PAD - j:SHIFT_PAD - j + tile, :]
            kj = kp_s[h, SHIFT_PAD - j:SHIFT_PAD - j + tile, :]
            vj = vp_s[h, SHIFT_PAD - j:SHIFT_PAD - j + tile, :]
            e = jnp.where(same_block[j], b[h] - bj, NEG)
            w = jnp.sum(qf[h] * kj * jnp.exp2(e), axis=-1, keepdims=True)
            o[h] = o[h] + w * vj

    st = [st_s[h] for h in heads]
    for h in heads:
        o[h] = o[h] + _qk((qf[h] * jnp.exp2(b[h])).astype(BF16), st[h].astype(BF16))
    for h in heads:
        b_last = bp_s[h, SHIFT_PAD + tile - 1:SHIFT_PAD + tile, :]
        kdec = (kf[h] * jnp.exp2(b_last - b[h])).astype(BF16)
        st_s[h] = jnp.exp2(b_last) * st[h] + lax.dot_general(
            vb[h], kdec, (((0,), (0,)), ((), ())), preferred_element_type=F32)

    for h in heads:
        on = o[h] * lax.rsqrt(jnp.mean(o[h] * o[h], axis=-1, keepdims=True) + NORM_EPS) * gn_ref[...]
        o_ref[0, :, cols[h]] = (on * gt_ref[0, :, cols[h]].astype(F32)).astype(BF16)

    @pl.when(t == nt - 1)
    def _():
        for h in range(hb):
            sfin_ref[0, h] = st_s[h].T


def _hgrn(q, lf, k, v, gt, g_norm, s0, tile, t_valid, hb):
    n, t, _ = q.shape
    nt = t // tile
    blk = pl.BlockSpec((1, tile, hb * LANES), lambda b, h, i: (b, i, h))
    st_blk = pl.BlockSpec((1, hb, HGRN_HEAD, HGRN_HEAD), lambda b, h, i: (b, h, 0, 0))
    pad = pltpu.VMEM((hb, SHIFT_PAD + tile, LANES), F32)
    return pl.pallas_call(
        functools.partial(_hgrn_kernel, tile=tile, t_valid=t_valid, nt=nt, hb=hb),
        grid=(n, N_HEADS_C // hb, nt),
        in_specs=[blk] * 5 + [_const_spec((1, LANES)), st_blk],
        out_specs=[blk, st_blk],
        out_shape=(jax.ShapeDtypeStruct((n, t, D_C), BF16),
                   jax.ShapeDtypeStruct((n, N_HEADS_C, HGRN_HEAD, HGRN_HEAD), F32)),
        scratch_shapes=[pltpu.VMEM((hb, HGRN_HEAD, HGRN_HEAD), F32),
                        pad, pad, pad,
                        pltpu.VMEM((tile, tile), BF16), pltpu.VMEM((tile, tile), jnp.int32)],
        compiler_params=_params("arbitrary", "arbitrary", "arbitrary"),
        name="hgrn",
    )(q, lf, k, v, gt, g_norm[None, :], s0)


def _tiles(n, t):
    rows = n * t
    tm = min(256, rows)
    tm_wide = min(512, rows)
    tc = min(256, t)
    th = min(128, t)
    return tm, tm_wide, tc, th, N_HEADS_C


def _trunk(x, pos, conv_hist, kv_cache, state0, t_valid, prm):
    n, t, _ = x.shape
    rows = n * t
    tm, tm_wide, tc, th, hb = _tiles(n, t)
    x2d = x.reshape(rows, D_MODEL)
    cos_t, sin_t = _rope_tables(pos)
    if cos_t.shape[0] < tm:
        cos_t = jnp.tile(cos_t, (tm // t, 1))
        sin_t = jnp.tile(sin_t, (tm // t, 1))
    kv_t = None
    if kv_cache is None:
        u, ga, q, k, v, gb, *kv_t = _even_inproj(x2d, prm["pre0"], prm["w_in_ab"], cos_t, sin_t, tm, seq_t=t)
    else:
        u, ga, q, k, v, gb = _even_inproj(x2d, prm["pre0"], prm["w_in_ab"], cos_t, sin_t, tm)
    shp = (n, t, D_A)
    ya, hist = _conv_module(u.reshape(shp), ga.reshape(shp), conv_hist, prm["conv_w"], prm["conv_b"],
                            prm["ln_g"], prm["ln_b"], tc, t_valid)
    if kv_cache is None:
        yb = _attn_prompt(q.reshape(shp), k.reshape(shp), v.reshape(shp), gb.reshape(shp))
    else:
        yb = _attn_sample(q.reshape(shp), k.reshape(shp), v.reshape(shp), kv_cache[0], kv_cache[1],
                          gb.reshape(shp), t_valid)
    x1 = _outproj([ya.reshape(rows, D_A), yb.reshape(rows, D_B)], prm["w_out_ab"], prm["post0"], x2d, tm_wide)
    qc, lf, kc, vc, gt = _odd_inproj(x1, prm["pre1"], prm["w_in_c"], prm["hgrn_lb"], 1, tm)
    wide = (n, t, D_C)
    oc, s_fin = _hgrn(qc.reshape(wide), lf.reshape(wide), kc.reshape(wide), vc.reshape(wide),
                      gt.reshape(wide), prm["gnorm"], state0, th, t_valid, hb)
    x2 = _outproj([oc.reshape(rows, D_C)], prm["w_out_c"], prm["post1"], x1, tm_wide)
    if kv_t is None:
        kv_t = (k.reshape(shp), v.reshape(shp))
    return x2.reshape(n, t, D_MODEL), hist, kv_t[0], kv_t[1], s_fin


def kernel(x_prompt, x_sample, cache_conv, cache_swa_k, cache_swa_v, state_hgrn, pre_norm, post_norm,
           w_in_ab, w_out_ab, conv_w, conv_b, conv_ln_g, conv_ln_b, w_in_c, w_out_c, hgrn_gnorm, hgrn_lb):
    n_p, seq_p, _ = x_prompt.shape
    n_s, seq_s, _ = x_sample.shape
    hist_rows = CONV_WIDTH - 1
    prm = dict(pre0=pre_norm[0], pre1=pre_norm[1], post0=post_norm[0], post1=post_norm[1],
               w_in_ab=w_in_ab[0].astype(BF16), w_out_ab=w_out_ab[0].astype(BF16),
               conv_w=conv_w[0], conv_b=conv_b[0], ln_g=conv_ln_g[0], ln_b=conv_ln_b[0],
               w_in_c=w_in_c[0].astype(BF16), w_out_c=w_out_c[0].astype(BF16),
               gnorm=hgrn_gnorm[0], hgrn_lb=hgrn_lb)

    pos_p = jnp.arange(seq_p, dtype=jnp.int32)
    yp, hist_p, k_p, v_p, s_p = _trunk(
        x_prompt, pos_p, jnp.zeros((n_p, HALO, D_A), F32), None,
        jnp.zeros((n_p, N_HEADS_C, HGRN_HEAD, HGRN_HEAD), F32), seq_p, prm)

    pad = SAMPLE_PAD - seq_s
    xs = jnp.pad(x_sample, ((0, 0), (0, pad), (0, 0)))
    pos_s = PAST_LEN + jnp.arange(SAMPLE_PAD, dtype=jnp.int32)
    hist_s = jnp.pad(cache_conv[0], ((0, 0), (HALO - hist_rows, 0), (0, 0)))
    kv = (jnp.transpose(cache_swa_k[0], (0, 2, 3, 1)), jnp.transpose(cache_swa_v[0], (0, 2, 3, 1)))
    ys, hist_s, k_s, v_s, s_s = _trunk(
        xs, pos_s, hist_s, kv, state_hgrn[0], seq_s, prm)

    heads = (N_HEADS_B, HEAD_DIM_B)
    return (yp, ys[:, :seq_s],
            hist_p[None, :, HALO - hist_rows:], hist_s[None, :, HALO - hist_rows:],
            jnp.transpose(k_p, (0, 3, 1, 2))[None], jnp.transpose(v_p, (0, 3, 1, 2))[None],
            k_s[:, :seq_s].reshape(1, n_s, seq_s, *heads), v_s[:, :seq_s].reshape(1, n_s, seq_s, *heads),
            s_p[None], s_s[None])
```

```python
import functools

import jax
import jax.numpy as jnp
from jax import lax
from jax.experimental import pallas as pl
from jax.experimental.pallas import tpu as pltpu

F32 = jnp.float32
BF16 = jnp.bfloat16

D_MODEL = 1024
D_A = 1024
D_B = 1024
CONV_WIDTH = 31
HALO = 32
HEAD_DIM_B = 64
N_HEADS_B = 16
LANES = 128
SUBLANES = 8
N_PAIRS = D_B // LANES
DSWA_CONFIGS = ((128, 1), (512, 4), (2048, 16))
SPAN = 128
ROPE_THETA = 10000.0
PAST_LEN = 16384
HGRN_HEAD = 128
N_HEADS_C = 16
D_C = N_HEADS_C * HGRN_HEAD
NORM_EPS = 1e-6
NEG = -1e30
LOG2_E = 1.4426950408889634
SAMPLE_PAD = 16
ATTN_GROUP = 4
VMEM_LIMIT = 56 * 1024 * 1024


def _silu(x):
    return x * jax.nn.sigmoid(x)


def _params(*sem):
    return pltpu.CompilerParams(dimension_semantics=sem, vmem_limit_bytes=VMEM_LIMIT)


def _const_spec(shape):
    return pl.BlockSpec(shape, lambda *_: (0,) * len(shape))


def _rope_table_kernel(pos_ref, invf_ref, sign_ref, cos_ref, sin_ref):
    ang = pos_ref[...] * invf_ref[...]
    cos_ref[...] = jnp.cos(ang)
    sin_ref[...] = jnp.sin(ang) * sign_ref[...]


def _rope_tables(pos):
    t = pos.shape[0]
    half = HEAD_DIM_B // 2
    inv_freq = ROPE_THETA ** (-jnp.arange(half, dtype=F32) / half)
    invf = jnp.tile(inv_freq, LANES // half)[None, :]
    lane = jnp.arange(LANES)
    sign = jnp.where(lane % HEAD_DIM_B < half, -1.0, 1.0).astype(F32)[None, :]
    posb = jnp.broadcast_to(pos.astype(F32)[:, None], (t, LANES))
    return pl.pallas_call(
        _rope_table_kernel,
        out_shape=(jax.ShapeDtypeStruct((t, LANES), F32),) * 2,
        name="rope_tables",
    )(posb, invf, sign)


def _even_inproj_kernel(*refs, conv_tiles):
    x_ref, g_ref, w_ref, cos_ref, sin_ref = refs[:5]
    if conv_tiles is None:
        u_ref, ga_ref, q_ref, k_ref, v_ref, gb_ref = refs[5:]
        kt_ref = vt_ref = None
    else:
        buf_ref, cw_ref, cb_ref, lg_ref, lb_ref = refs[5:10]
        ya_ref, tail_ref, q_ref, k_ref, v_ref, gb_ref, kt_ref, vt_ref = refs[10:18]
        (ext_ref,) = refs[18:]
        if conv_tiles > 1:
            @pl.when(pl.program_id(0) == 0)
            def _():
                ext_ref[...] = jnp.zeros(ext_ref.shape, F32)
    x = x_ref[...]
    ms = jnp.mean(x * x, axis=-1, keepdims=True)
    h = (x * lax.rsqrt(ms + NORM_EPS) * g_ref[...]).astype(BF16)

    def seg(j):
        return jnp.dot(h, w_ref[:, j * D_A:(j + 1) * D_A], preferred_element_type=F32)

    u = seg(0) * jax.nn.sigmoid(seg(1))
    gate = _silu(seg(2))
    if conv_tiles is None:
        u_ref[...] = u
        ga_ref[...] = gate.astype(BF16)

    cos = cos_ref[...]
    sin = sin_ref[...]
    lane = lax.broadcasted_iota(jnp.int32, cos.shape, 1)
    first_half = (lane % HEAD_DIM_B) < (HEAD_DIM_B // 2)

    def emit(val, out_ref, rotate, t_ref):
        for c in range(N_PAIRS):
            xc = val[:, c * LANES:(c + 1) * LANES]
            if rotate:
                partner = jnp.where(first_half,
                                    pltpu.roll(xc, LANES - HEAD_DIM_B // 2, 1),
                                    pltpu.roll(xc, HEAD_DIM_B // 2, 1))
                xc = xc * cos + partner * sin
            out_ref[:, c * LANES:(c + 1) * LANES] = xc
            if t_ref is not None:
                xt = xc.T
                t_ref[0, 2 * c] = xt[0:HEAD_DIM_B]
                t_ref[0, 2 * c + 1] = xt[HEAD_DIM_B:LANES]

    def emit_gate():
        gb_ref[...] = _silu(seg(6)).astype(BF16)

    rest = [lambda: emit(seg(3), q_ref, True, None), lambda: emit(seg(4), k_ref, True, kt_ref),
            lambda: emit(seg(5), v_ref, False, vt_ref), emit_gate]
    if conv_tiles is None:
        for work in rest:
            work()
    else:
        tc = x.shape[0]
        ya = _conv_tile(pl.program_id(0) % conv_tiles, u, gate, buf_ref, cw_ref, cb_ref, lg_ref, lb_ref,
                        tail_ref, ext_ref, tc=tc, t_valid=tc, nt=conv_tiles, between=rest)
        ya_ref[...] = ya.astype(BF16)


def _even_inproj(x2d, pre_g, w_bf16, cos_t, sin_t, tm, conv=None):
    rows = x2d.shape[0]
    nt_tab = cos_t.shape[0] // tm
    row_spec = pl.BlockSpec((tm, D_MODEL), lambda i: (i, 0))
    tab_spec = pl.BlockSpec((tm, LANES), lambda i: (i % nt_tab, 0))
    f32_out = jax.ShapeDtypeStruct((rows, D_A), F32)
    bf_out = jax.ShapeDtypeStruct((rows, D_A), BF16)
    in_specs = [row_spec, _const_spec((1, D_MODEL)),
                pl.BlockSpec(w_bf16.shape, lambda i: (0, 0), pipeline_mode=pl.Buffered(1)),
                tab_spec, tab_spec]
    operands = [x2d, pre_g[None, :], w_bf16, cos_t, sin_t]
    if conv is None:
        conv_tiles = None
        out_specs = [row_spec] * 6
        out_shape = (f32_out, bf_out, f32_out, f32_out, f32_out, bf_out)
        scratch = []
    else:
        seq, hist, conv_w, conv_b, ln_g, ln_b = conv
        n = rows // seq
        conv_tiles = seq // tm
        halo = pl.BlockSpec((1, HALO, D_A), lambda i: (i // conv_tiles, 0, 0))
        t_spec = pl.BlockSpec((1, N_HEADS_B, HEAD_DIM_B, tm), lambda i: (i // conv_tiles, 0, 0, i % conv_tiles))
        t_out = jax.ShapeDtypeStruct((n, N_HEADS_B, HEAD_DIM_B, seq), F32)
        in_specs += [halo, _const_spec((CONV_WIDTH, D_A))] + [_const_spec((1, D_A))] * 3
        operands += [hist, conv_w, conv_b[None, :], ln_g[None, :], ln_b[None, :]]
        out_specs = [row_spec, halo] + [row_spec] * 4 + [t_spec] * 2
        out_shape = (bf_out, jax.ShapeDtypeStruct((n, HALO, D_A), F32),
                     f32_out, f32_out, f32_out, bf_out, t_out, t_out)
        scratch = _conv_scratch(tm)
    return pl.pallas_call(
        functools.partial(_even_inproj_kernel, conv_tiles=conv_tiles),
        grid=(rows // tm,),
        in_specs=in_specs,
        out_specs=out_specs,
        out_shape=out_shape,
        scratch_shapes=scratch,
        compiler_params=_params("arbitrary"),
        name="even_inproj",
    )(*operands)


def _conv_tile(t, u, gate, buf_ref, w_ref, cb_ref, lg_ref, lb_ref, tail_ref, ext_ref,
               *, tc, t_valid, nt, between=()):
    between = list(between)
    n_chunks = D_A // LANES
    if nt > 1:
        ext_ref[0:HALO, :] = jnp.where(t == 0, buf_ref[0], ext_ref[tc:tc + HALO, :])
    else:
        ext_ref[0:HALO, :] = buf_ref[0]
    ext_ref[HALO:HALO + tc, :] = u
    ext_ref[HALO + tc:HALO + tc + SUBLANES, :] = jnp.zeros((SUBLANES, D_A), F32)
    first = HALO - (CONV_WIDTH - 1)
    chunks = []
    for c in range(n_chunks):
        for j, work in enumerate(between):
            if j * n_chunks // len(between) == c:
                work()
        cols = slice(c * LANES, (c + 1) * LANES)
        acc = jnp.zeros((tc, LANES), F32)
        for rho in range(SUBLANES):
            part = None
            for k in range(CONV_WIDTH):
                if (first + k) % SUBLANES == rho:
                    base = first + k - rho
                    term = ext_ref[base:base + tc + SUBLANES, cols] * w_ref[k:k + 1, cols]
                    part = term if part is None else part + term
            if part is None:
                continue
            acc = acc + part[rho:rho + tc]
        chunks.append(acc + cb_ref[:, cols])
    y = jnp.concatenate(chunks, axis=1)
    yc = y - jnp.mean(y, axis=-1, keepdims=True)
    yn = yc * lax.rsqrt(jnp.mean(yc * yc, axis=-1, keepdims=True) + NORM_EPS)
    yn = yn * lg_ref[...] + lb_ref[...]
    tail_ref[0] = ext_ref[t_valid:t_valid + HALO, :]
    return _silu(yn) * gate


def _conv_scratch(tc):
    return [pltpu.VMEM((HALO + tc + SUBLANES, D_A), F32)]


def _conv_kernel(u_ref, ga_ref, buf_ref, w_ref, cb_ref, lg_ref, lb_ref,
                 ya_ref, tail_ref, ext_ref, *, tc, t_valid, nt):
    if nt > 1:
        @pl.when((pl.program_id(0) == 0) & (pl.program_id(1) == 0))
        def _():
            ext_ref[...] = jnp.zeros(ext_ref.shape, F32)
    ya = _conv_tile(pl.program_id(1), u_ref[0], ga_ref[0].astype(F32), buf_ref, w_ref, cb_ref, lg_ref, lb_ref,
                    tail_ref, ext_ref, tc=tc, t_valid=t_valid, nt=nt)
    ya_ref[0] = ya.astype(BF16)


def _conv_module(u, ga, buf, conv_w, conv_b, ln_g, ln_b, tc, t_valid):
    n, t, _ = u.shape
    nt = t // tc
    blk = pl.BlockSpec((1, tc, D_A), lambda b, i: (b, i, 0))
    halo = pl.BlockSpec((1, HALO, D_A), lambda b, i: (b, 0, 0))
    last_valid = t_valid - (nt - 1) * tc
    return pl.pallas_call(
        functools.partial(_conv_kernel, tc=tc, t_valid=last_valid, nt=nt),
        grid=(n, nt),
        in_specs=[blk, blk, halo, _const_spec((CONV_WIDTH, D_A)),
                  _const_spec((1, D_A)), _const_spec((1, D_A)), _const_spec((1, D_A))],
        out_specs=[blk, halo],
        out_shape=(jax.ShapeDtypeStruct((n, t, D_A), BF16),
                   jax.ShapeDtypeStruct((n, HALO, D_A), F32)),
        scratch_shapes=_conv_scratch(tc),
        compiler_params=_params("arbitrary", "arbitrary"),
        name="conv_module",
    )(u, ga, buf, conv_w, conv_b[None, :], ln_g[None, :], ln_b[None, :])


def _qk(q, k):
    return lax.dot_general(q, k, (((1,), (1,)), ((), ())), preferred_element_type=F32)


def _attn_prompt_kernel(q_ref, k_ref, v_ref, gb_ref, out_ref, q_s, k_s, v_s, o_s, m_s, l_s, *, seq):
    nblk = seq // SPAN
    scale = HEAD_DIM_B ** -0.5 * LOG2_E
    lo = lax.broadcasted_iota(jnp.int32, (SPAN, LANES), 1) < HEAD_DIM_B

    def rows(c, blk):
        dil = DSWA_CONFIGS[c][1]
        per_seq = nblk // dil
        return dil * SPAN * (blk % per_seq) + blk // per_seq, dil

    def take(ref, start, stride):
        if stride == 1:
            return ref[0, start:start + SPAN, :]
        return ref[0, pl.ds(start, SPAN, stride=stride), :]

    for c in range(len(DSWA_CONFIGS)):
        for blk in range(nblk):
            start, stride = rows(c, blk)
            qb = take(q_ref, start, stride) * scale
            q_s[c, blk, 0:SPAN, :] = jnp.where(lo, qb, 0.0).astype(BF16)
            q_s[c, blk, SPAN:2 * SPAN, :] = jnp.where(lo, 0.0, qb).astype(BF16)
            k_s[c, blk] = take(k_ref, start, stride).astype(BF16)
            vb = take(v_ref, start, stride)
            v_s[c, blk, 0] = jnp.where(lo, vb, 1.0).astype(BF16)
            v_s[c, blk, 1] = jnp.where(lo, 1.0, vb).astype(BF16)

    row = lax.broadcasted_iota(jnp.int32, (2 * SPAN, SPAN), 0) % SPAN
    col = lax.broadcasted_iota(jnp.int32, (2 * SPAN, SPAN), 1)
    upper = col >= row
    diag = col == row
    lower = col < row

    def has_prev(c, blk):
        return blk % (nblk // DSWA_CONFIGS[c][1]) > 0

    def scores(c, blk):
        q2 = q_s[c, blk]
        s_own = _qk(q2, k_s[c, blk])
        s_far = _qk(q2, k_s[c, blk - 1]) if has_prev(c, blk) else NEG
        return s_own, s_far

    def probabilities(s_own, s_far):
        s_self = jnp.sum(jnp.where(diag, s_own, 0.0), axis=-1, keepdims=True)
        s = jnp.where(upper, s_far, s_own)
        m = jnp.maximum(jnp.max(s, axis=-1, keepdims=True), s_self)
        p = jnp.exp2(s - m)
        p_self = jnp.exp2(s_self - m)
        p_own = jnp.where(lower, p, jnp.where(diag, p_self, 0.0)).astype(BF16)
        p_far = jnp.where(upper, p, 0.0).astype(BF16)
        return p_own, p_far, m

    def finish(c, blk, p_own, p_far, m):
        o = []
        for h in range(2):
            ph = slice(h * SPAN, (h + 1) * SPAN)
            o_h = jnp.dot(p_own[ph], v_s[c, blk, h], preferred_element_type=F32)
            if has_prev(c, blk):
                o_h = o_h + jnp.dot(p_far[ph], v_s[c, blk - 1, h], preferred_element_type=F32)
            o.append(o_h)
        m = jnp.broadcast_to(m, (2 * SPAN, LANES))
        start, stride = rows(c, blk)
        dst = pl.ds(start, SPAN, stride=stride) if stride > 1 else pl.ds(start, SPAN)
        o_s[c, dst, :] = jnp.where(lo, o[0], o[1])
        m_s[c, dst, :] = jnp.where(lo, m[0:SPAN], m[SPAN:2 * SPAN])
        l_s[c, dst, :] = jnp.where(lo, o[1], o[0])

    todo = [(c, blk) for c in range(len(DSWA_CONFIGS)) for blk in range(nblk)]
    groups = [todo[i:i + ATTN_GROUP] for i in range(0, len(todo), ATTN_GROUP)]
    pending = [scores(*cb) for cb in groups[0]]
    for gi, group in enumerate(groups):
        ahead = [scores(*cb) for cb in groups[gi + 1]] if gi + 1 < len(groups) else []
        probs = [probabilities(*sc) for sc in pending]
        for cb, pr in zip(group, probs):
            finish(*cb, *pr)
        pending = ahead

    ma, mb, mc = m_s[0], m_s[1], m_s[2]
    mx = jnp.maximum(jnp.maximum(ma, mb), mc)
    ea, eb, ec = jnp.exp2(ma - mx), jnp.exp2(mb - mx), jnp.exp2(mc - mx)
    la, lb, lc = (pltpu.roll(l_s[c], HEAD_DIM_B, 1) for c in range(3))
    merged = (ea * o_s[0] + eb * o_s[1] + ec * o_s[2]) / (ea * la + eb * lb + ec * lc)
    out_ref[0] = (merged * gb_ref[0].astype(F32)).astype(BF16)


def _attn_prompt(q, k, v, gb):
    n, seq, _ = q.shape
    nblk = seq // SPAN
    assert all(w // d == SPAN and nblk % d == 0 for w, d in DSWA_CONFIGS)
    ncfg = len(DSWA_CONFIGS)
    blk = pl.BlockSpec((1, seq, LANES), lambda b, h: (b, 0, h))
    return pl.pallas_call(
        functools.partial(_attn_prompt_kernel, seq=seq),
        grid=(n, N_PAIRS),
        in_specs=[blk] * 4,
        out_specs=blk,
        out_shape=jax.ShapeDtypeStruct((n, seq, D_B), BF16),
        scratch_shapes=[pltpu.VMEM((ncfg, nblk, 2 * SPAN, LANES), BF16),
                        pltpu.VMEM((ncfg, nblk, SPAN, LANES), BF16),
                        pltpu.VMEM((ncfg, nblk, 2, SPAN, LANES), BF16),
                        pltpu.VMEM((ncfg, seq, LANES), F32),
                        pltpu.VMEM((ncfg, seq, LANES), F32),
                        pltpu.VMEM((ncfg, seq, LANES), F32)],
        compiler_params=_params("arbitrary", "arbitrary"),
        name="attn_prompt",
    )(q, k, v, gb)


def _attn_sample_kernel(q_ref, kn_ref, vn_ref, kt_ref, vt_ref, gb_ref, out_ref, *, t_new, n_past, heads):
    nq = SAMPLE_PAD
    width = heads * HEAD_DIM_B
    scale = HEAD_DIM_B ** -0.5
    q_all = q_ref[0] * scale
    k_new = kn_ref[0]
    v_new = vn_ref[0]

    lane = lax.broadcasted_iota(jnp.int32, (nq, width), 1)
    qm = jnp.concatenate([jnp.where(lane // HEAD_DIM_B == h, q_all, 0.0) for h in range(heads)], axis=0)
    s_new = _qk(qm.astype(BF16), k_new.astype(BF16))
    t_row = lax.broadcasted_iota(jnp.int32, (nq, nq), 0)
    t_col = lax.broadcasted_iota(jnp.int32, (nq, nq), 1)

    dist = (n_past + lax.broadcasted_iota(jnp.int32, (nq, n_past), 0)
            - lax.broadcasted_iota(jnp.int32, (nq, n_past), 1))
    far = 1 << 30
    masks, starts = [], []
    for window, dil in DSWA_CONFIGS:
        start = n_past - window
        d = dist[:, start:]
        masks.append(jnp.where((d & (dil - 1)) == 0, d, far) <= window)
        starts.append(start)

    new_ok = [jnp.where(((t_row - t_col) & (dil - 1)) == 0, t_row - t_col, -1) >= 0 for _, dil in DSWA_CONFIGS]
    ncfg = len(DSWA_CONFIGS)
    cols = [slice(h * HEAD_DIM_B, (h + 1) * HEAD_DIM_B) for h in range(heads)]

    s_buf = [jnp.dot(q_all[:, cols[h]].astype(BF16), kt_ref[0, h].astype(BF16), preferred_element_type=F32)
             for h in range(heads)]
    stats = []
    for h in range(heads):
        sn = s_new[h * nq:(h + 1) * nq, :]
        per_cfg = []
        for c in range(ncfg):
            sc = jnp.where(masks[c], s_buf[h][:, starts[c]:], NEG)
            snm = jnp.where(new_ok[c], sn, NEG)
            m = jnp.maximum(jnp.max(sc, axis=-1, keepdims=True), jnp.max(snm, axis=-1, keepdims=True))
            p = jnp.exp(sc - m)
            pn = jnp.exp(snm - m)
            l = jnp.sum(p, axis=-1, keepdims=True) + jnp.sum(pn, axis=-1, keepdims=True)
            per_cfg.append((p.astype(BF16), pn, l, m + jnp.log(l)))
        stats.append(per_cfg)
    outs = []
    for h in range(heads):
        vt = vt_ref[0, h].astype(BF16)
        vnh = v_new[:, cols[h]]
        acc = []
        for c in range(ncfg):
            p, pn, l, _ = stats[h][c]
            o = _qk(p, vt[:, starts[c]:])
            for tn in range(t_new):
                o = o + pn[:, tn:tn + 1] * vnh[tn:tn + 1, :]
            acc.append(o / l)
        den = [stats[h][c][3] for c in range(ncfg)]
        mx = jnp.maximum(jnp.maximum(den[0], den[1]), den[2])
        e = [jnp.exp(d - mx) for d in den]
        outs.append((e[0] * acc[0] + e[1] * acc[1] + e[2] * acc[2]) / (e[0] + e[1] + e[2]))
    merged = jnp.concatenate(outs, axis=1)
    out_ref[0] = (merged * gb_ref[0].astype(F32)).astype(BF16)


def _attn_sample(q, k_new, v_new, kt_cache, vt_cache, gb, t_new, heads=8):
    n, _, _, n_past = kt_cache.shape
    assert n_past >= DSWA_CONFIGS[2][0] and t_new <= min(d for _, d in DSWA_CONFIGS[1:]) and t_new <= SAMPLE_PAD
    width = heads * HEAD_DIM_B
    new_blk = pl.BlockSpec((1, SAMPLE_PAD, width), lambda b, h: (b, 0, h))
    cache_blk = pl.BlockSpec((1, heads, HEAD_DIM_B, n_past), lambda b, h: (b, h, 0, 0))
    return pl.pallas_call(
        functools.partial(_attn_sample_kernel, t_new=t_new, n_past=n_past, heads=heads),
        grid=(n, N_HEADS_B // heads),
        in_specs=[new_blk] * 3 + [cache_blk] * 2 + [new_blk],
        out_specs=new_blk,
        out_shape=jax.ShapeDtypeStruct((n, SAMPLE_PAD, D_B), BF16),
        compiler_params=_params("arbitrary", "arbitrary"),
        name="attn_sample",
    )(q, k_new, v_new, kt_cache, vt_cache, gb)


def _outproj_kernel(*refs, n_in):
    ins, (w_ref, g_ref, x_ref, out_ref) = refs[:n_in], refs[n_in:]
    y = None
    off = 0
    for r in ins:
        width = r.shape[1]
        part = jnp.dot(r[...], w_ref[off:off + width, :], preferred_element_type=F32)
        y = part if y is None else y + part
        off += width
    yn = y * lax.rsqrt(jnp.mean(y * y, axis=-1, keepdims=True) + NORM_EPS) * g_ref[...]
    out_ref[...] = x_ref[...] + yn


def _outproj(parts, w_bf16, post_g, x2d, tm):
    rows = x2d.shape[0]
    row_spec = pl.BlockSpec((tm, D_MODEL), lambda i: (i, 0))
    return pl.pallas_call(
        functools.partial(_outproj_kernel, n_in=len(parts)),
        grid=(rows // tm,),
        in_specs=[pl.BlockSpec((tm, p.shape[1]), lambda i: (i, 0)) for p in parts]
        + [pl.BlockSpec(w_bf16.shape, lambda i: (0, 0), pipeline_mode=pl.Buffered(1)),
           _const_spec((1, D_MODEL)), row_spec],
        out_specs=row_spec,
        out_shape=jax.ShapeDtypeStruct((rows, D_MODEL), F32),
        compiler_params=_params("arbitrary"),
        name="outproj",
    )(*parts, w_bf16, post_g[None, :], x2d)


def _odd_inproj_kernel(x_ref, g_ref, w_ref, lbp_ref, q_ref, lf_ref, k_ref, v_ref, gt_ref, *, layer):
    x = x_ref[...]
    ms = jnp.mean(x * x, axis=-1, keepdims=True)
    h = (x * lax.rsqrt(ms + NORM_EPS) * g_ref[...]).astype(BF16)

    def seg(j):
        return jnp.dot(h, w_ref[:, j * D_C:(j + 1) * D_C], preferred_element_type=F32)

    lbp = lbp_ref[...]
    e = jnp.exp(lbp - jnp.max(lbp, axis=0, keepdims=True))
    sm = e / jnp.sum(e, axis=0, keepdims=True)
    lb = jnp.sum(sm[1:layer + 1], axis=0, keepdims=True)

    q_ref[...] = _silu(seg(0)).astype(BF16)
    f = lb + (1.0 - lb) * jax.nn.sigmoid(seg(1))
    lf_ref[...] = jnp.log(f)
    k_ref[...] = (1.0 - f).astype(BF16)
    v_ref[...] = seg(2).astype(BF16)
    gt_ref[...] = _silu(seg(3)).astype(BF16)


def _odd_inproj(x2d, pre_g, w_bf16, hgrn_lb, layer, tm):
    rows = x2d.shape[0]
    row_spec = pl.BlockSpec((tm, D_MODEL), lambda i: (i, 0))
    wide = pl.BlockSpec((tm, D_C), lambda i: (i, 0))
    bf = jax.ShapeDtypeStruct((rows, D_C), BF16)
    return pl.pallas_call(
        functools.partial(_odd_inproj_kernel, layer=layer),
        grid=(rows // tm,),
        in_specs=[row_spec, _const_spec((1, D_MODEL)),
                  pl.BlockSpec(w_bf16.shape, lambda i: (0, 0), pipeline_mode=pl.Buffered(1)),
                  _const_spec(hgrn_lb.shape)],
        out_specs=[wide] * 5,
        out_shape=(bf, jax.ShapeDtypeStruct((rows, D_C), F32), bf, bf, bf),
        compiler_params=_params("arbitrary"),
        name="odd_inproj",
    )(x2d, pre_g[None, :], w_bf16, hgrn_lb)


DIAG = 4
SHIFT_PAD = 8

def _hgrn_kernel(q_ref, lf_ref, k_ref, v_ref, gt_ref, gn_ref, s0_ref,
                 o_ref, sfin_ref, st_s, bp_s, kp_s, vp_s, tri_s, split_s, *, tile, t_valid, nt, hb):
    t = pl.program_id(2)

    @pl.when(t == 0)
    def _():
        for h in range(hb):
            st_s[h] = s0_ref[0, h].T

    @pl.when((pl.program_id(0) == 0) & (pl.program_id(1) == 0) & (t == 0))
    def _():
        rr = lax.broadcasted_iota(jnp.int32, (tile, tile), 0)
        cc = lax.broadcasted_iota(jnp.int32, (tile, tile), 1)
        tri_s[...] = jnp.where(rr >= cc, 1.0, 0.0).astype(BF16)
        split_s[...] = 31 - lax.clz(jnp.where(rr > cc, rr ^ cc, 0))

    row = lax.broadcasted_iota(jnp.int32, (tile, LANES), 0)
    tri = tri_s[...]
    split_bit = split_s[...]
    zpad = jnp.zeros((SHIFT_PAD, LANES), F32)
    same_block = [(row % DIAG) >= j for j in range(DIAG)]
    levels = []
    m = DIAG
    while 2 * m <= tile:
        levels.append(m)
        m *= 2

    heads = range(hb)
    cols = [slice(h * LANES, (h + 1) * LANES) for h in heads]
    qf, kf, vb, pieces = [], [], [], []
    for h in heads:
        lf = lf_ref[0, :, cols[h]] * LOG2_E
        k_h = k_ref[0, :, cols[h]].astype(F32)
        if t_valid < tile:
            lf = jnp.where(row < t_valid, lf, 0.0)
            k_h = jnp.where(row < t_valid, k_h, 0.0)
        kf.append(k_h)
        qf.append(q_ref[0, :, cols[h]].astype(F32))
        vb.append(v_ref[0, :, cols[h]])
        p1 = lf.astype(BF16)
        r1 = lf - p1.astype(F32)
        p2 = r1.astype(BF16)
        pieces.append((p1, p2, (r1 - p2.astype(F32)).astype(BF16)))

    b = []
    for h in heads:
        b_h = (jnp.dot(tri, pieces[h][0], preferred_element_type=F32)
               + jnp.dot(tri, pieces[h][1], preferred_element_type=F32)
               + jnp.dot(tri, pieces[h][2], preferred_element_type=F32))
        b.append(b_h)
        bp_s[h, 0:SHIFT_PAD, :] = zpad
        kp_s[h, 0:SHIFT_PAD, :] = zpad
        vp_s[h, 0:SHIFT_PAD, :] = zpad
        bp_s[h, SHIFT_PAD:SHIFT_PAD + tile, :] = b_h
        kp_s[h, SHIFT_PAD:SHIFT_PAD + tile, :] = kf[h]
        vp_s[h, SHIFT_PAD:SHIFT_PAD + tile, :] = vb[h].astype(F32)

    attn = [jnp.zeros((tile, tile), F32) for _ in heads]
    for m in levels:
        nb = tile // (2 * m)
        for h in heads:
            if nb > 1:
                mid = bp_s[h, pl.ds(SHIFT_PAD + m - 1, nb, stride=2 * m), :]
                mid = jnp.broadcast_to(mid[:, None, :], (nb, 2 * m, LANES)).reshape(tile, LANES)
            else:
                mid = jnp.broadcast_to(bp_s[h, SHIFT_PAD + m - 1:SHIFT_PAD + m, :], (tile, LANES))
            decay = jnp.exp2(-jnp.abs(b[h] - mid))
            attn[h] = jnp.where(split_bit == m.bit_length() - 1,
                                _qk((qf[h] * decay).astype(BF16), (kf[h] * decay).astype(BF16)), attn[h])
    o = [jnp.dot(attn[h].astype(BF16), vb[h], preferred_element_type=F32) for h in heads]

    for h in heads:
        vf = vp_s[h, SHIFT_PAD:SHIFT_PAD + tile, :]
        o[h] = o[h] + jnp.sum(qf[h] * kf[h], axis=-1, keepdims=True) * vf
        for j in range(1, DIAG):
            bj = bp_s[h, SHIFT_PAD - j:SHIFT_PAD - j + tile, :]
            kj = kp_s[h, SHIFT_PAD - j:SHIFT_PAD - j + tile, :]
            vj = vp_s[h, SHIFT_PAD - j:SHIFT_PAD - j + tile, :]
            e = jnp.where(same_block[j], b[h] - bj, NEG)
            w = jnp.sum(qf[h] * kj * jnp.exp2(e), axis=-1, keepdims=True)
            o[h] = o[h] + w * vj

    st = [st_s[h] for h in heads]
    for h in heads:
        o[h] = o[h] + _qk((qf[h] * jnp.exp2(b[h])).astype(BF16), st[h].astype(BF16))
    for h in heads:
        b_last = bp_s[h, SHIFT_PAD + tile - 1:SHIFT_PAD + tile, :]
        kdec = (kf[h] * jnp.exp2(b_last - b[h])).astype(BF16)
        st_s[h] = jnp.exp2(b_last) * st[h] + lax.dot_general(
            vb[h], kdec, (((0,), (0,)), ((), ())), preferred_element_type=F32)

    for h in heads:
        on = o[h] * lax.rsqrt(jnp.mean(o[h] * o[h], axis=-1, keepdims=True) + NORM_EPS) * gn_ref[...]
        o_ref[0, :, cols[h]] = (on * gt_ref[0, :, cols[h]].astype(F32)).astype(BF16)

    @pl.when(t == nt - 1)
    def _():
        for h in range(hb):
            sfin_ref[0, h] = st_s[h].T


def _hgrn(q, lf, k, v, gt, g_norm, s0, tile, t_valid, hb):
    n, t, _ = q.shape
    nt = t // tile
    blk = pl.BlockSpec((1, tile, hb * LANES), lambda b, h, i: (b, i, h))
    st_blk = pl.BlockSpec((1, hb, HGRN_HEAD, HGRN_HEAD), lambda b, h, i: (b, h, 0, 0))
    pad = pltpu.VMEM((hb, SHIFT_PAD + tile, LANES), F32)
    return pl.pallas_call(
        functools.partial(_hgrn_kernel, tile=tile, t_valid=t_valid, nt=nt, hb=hb),
        grid=(n, N_HEADS_C // hb, nt),
        in_specs=[blk] * 5 + [_const_spec((1, LANES)), st_blk],
        out_specs=[blk, st_blk],
        out_shape=(jax.ShapeDtypeStruct((n, t, D_C), BF16),
                   jax.ShapeDtypeStruct((n, N_HEADS_C, HGRN_HEAD, HGRN_HEAD), F32)),
        scratch_shapes=[pltpu.VMEM((hb, HGRN_HEAD, HGRN_HEAD), F32),
                        pad, pad, pad,
                        pltpu.VMEM((tile, tile), BF16), pltpu.VMEM((tile, tile), jnp.int32)],
        compiler_params=_params("arbitrary", "arbitrary", "arbitrary"),
        name="hgrn",
    )(q, lf, k, v, gt, g_norm[None, :], s0)


def _tiles(n, t):
    rows = n * t
    tm = min(256, rows)
    tm_wide = min(512, rows)
    tc = min(256, t)
    th = min(128, t)
    return tm, tm_wide, tc, th, N_HEADS_C


def _trunk(x, pos, conv_hist, kv_cache, state0, t_valid, prm):
    n, t, _ = x.shape
    rows = n * t
    tm, tm_wide, tc, th, hb = _tiles(n, t)
    x2d = x.reshape(rows, D_MODEL)
    cos_t, sin_t = _rope_tables(pos)
    if cos_t.shape[0] < tm:
        cos_t = jnp.tile(cos_t, (tm // t, 1))
        sin_t = jnp.tile(sin_t, (tm // t, 1))
    shp = (n, t, D_A)
    conv_prm = (prm["conv_w"], prm["conv_b"], prm["ln_g"], prm["ln_b"])
    if kv_cache is None:
        assert t_valid == t and t % tm == 0
        ya, hist, q, k, v, gb, *kv_t = _even_inproj(x2d, prm["pre0"], prm["w_in_ab"], cos_t, sin_t, tm,
                                                    conv=(t, conv_hist) + conv_prm)
        yb = _attn_prompt(q.reshape(shp), k.reshape(shp), v.reshape(shp), gb.reshape(shp))
    else:
        kv_t = None
        u, ga, q, k, v, gb = _even_inproj(x2d, prm["pre0"], prm["w_in_ab"], cos_t, sin_t, tm)
        ya, hist = _conv_module(u.reshape(shp), ga.reshape(shp), conv_hist, *conv_prm, tc, t_valid)
        yb = _attn_sample(q.reshape(shp), k.reshape(shp), v.reshape(shp), kv_cache[0], kv_cache[1],
                          gb.reshape(shp), t_valid)
    x1 = _outproj([ya.reshape(rows, D_A), yb.reshape(rows, D_B)], prm["w_out_ab"], prm["post0"], x2d, tm_wide)
    qc, lf, kc, vc, gt = _odd_inproj(x1, prm["pre1"], prm["w_in_c"], prm["hgrn_lb"], 1, tm)
    wide = (n, t, D_C)
    oc, s_fin = _hgrn(qc.reshape(wide), lf.reshape(wide), kc.reshape(wide), vc.reshape(wide),
                      gt.reshape(wide), prm["gnorm"], state0, th, t_valid, hb)
    x2 = _outproj([oc.reshape(rows, D_C)], prm["w_out_c"], prm["post1"], x1, tm_wide)
    if kv_t is None:
        kv_t = (k.reshape(shp), v.reshape(shp))
    return x2.reshape(n, t, D_MODEL), hist, kv_t[0], kv_t[1], s_fin


def kernel(x_prompt, x_sample, cache_conv, cache_swa_k, cache_swa_v, state_hgrn, pre_norm, post_norm,
           w_in_ab, w_out_ab, conv_w, conv_b, conv_ln_g, conv_ln_b, w_in_c, w_out_c, hgrn_gnorm, hgrn_lb):
    n_p, seq_p, _ = x_prompt.shape
    n_s, seq_s, _ = x_sample.shape
    hist_rows = CONV_WIDTH - 1
    prm = dict(pre0=pre_norm[0], pre1=pre_norm[1], post0=post_norm[0], post1=post_norm[1],
               w_in_ab=w_in_ab[0].astype(BF16), w_out_ab=w_out_ab[0].astype(BF16),
               conv_w=conv_w[0], conv_b=conv_b[0], ln_g=conv_ln_g[0], ln_b=conv_ln_b[0],
               w_in_c=w_in_c[0].astype(BF16), w_out_c=w_out_c[0].astype(BF16),
               gnorm=hgrn_gnorm[0], hgrn_lb=hgrn_lb)

    pos_p = jnp.arange(seq_p, dtype=jnp.int32)
    yp, hist_p, k_p, v_p, s_p = _trunk(
        x_prompt, pos_p, jnp.zeros((n_p, HALO, D_A), F32), None,
        jnp.zeros((n_p, N_HEADS_C, HGRN_HEAD, HGRN_HEAD), F32), seq_p, prm)

    pad = SAMPLE_PAD - seq_s
    xs = jnp.pad(x_sample, ((0, 0), (0, pad), (0, 0)))
    pos_s = PAST_LEN + jnp.arange(SAMPLE_PAD, dtype=jnp.int32)
    hist_s = jnp.pad(cache_conv[0], ((0, 0), (HALO - hist_rows, 0), (0, 0)))
    kv = (jnp.transpose(cache_swa_k[0], (0, 2, 3, 1)), jnp.transpose(cache_swa_v[0], (0, 2, 3, 1)))
    ys, hist_s, k_s, v_s, s_s = _trunk(
        xs, pos_s, hist_s, kv, state_hgrn[0], seq_s, prm)

    heads = (N_HEADS_B, HEAD_DIM_B)
    return (yp, ys[:, :seq_s],
            hist_p[None, :, HALO - hist_rows:], hist_s[None, :, HALO - hist_rows:],
            jnp.transpose(k_p, (0, 3, 1, 2))[None], jnp.transpose(v_p, (0, 3, 1, 2))[None],
            k_s[:, :seq_s].reshape(1, n_s, seq_s, *heads), v_s[:, :seq_s].reshape(1, n_s, seq_s, *heads),
            s_p[None], s_s[None])
```

```python
import functools

import jax
import jax.numpy as jnp
from jax import lax
from jax.experimental import pallas as pl
from jax.experimental.pallas import tpu as pltpu

F32 = jnp.float32
BF16 = jnp.bfloat16

D_MODEL = 1024
D_A = 1024
D_B = 1024
CONV_WIDTH = 31
HALO = 32
HEAD_DIM_B = 64
N_HEADS_B = 16
LANES = 128
SUBLANES = 8
N_PAIRS = D_B // LANES
DSWA_CONFIGS = ((128, 1), (512, 4), (2048, 16))
SPAN = 128
ROPE_THETA = 10000.0
PAST_LEN = 16384
HGRN_HEAD = 128
N_HEADS_C = 16
D_C = N_HEADS_C * HGRN_HEAD
NORM_EPS = 1e-6
NEG = -1e30
LOG2_E = 1.4426950408889634
SAMPLE_PAD = 16
ATTN_GROUP = 4
VMEM_LIMIT = 56 * 1024 * 1024


def _silu(x):
    return x * jax.nn.sigmoid(x)


def _params(*sem):
    return pltpu.CompilerParams(dimension_semantics=sem, vmem_limit_bytes=VMEM_LIMIT)


def _const_spec(shape):
    return pl.BlockSpec(shape, lambda *_: (0,) * len(shape))


def _rope_table_kernel(pos_ref, invf_ref, sign_ref, cos_ref, sin_ref):
    ang = pos_ref[...] * invf_ref[...]
    cos_ref[...] = jnp.cos(ang)
    sin_ref[...] = jnp.sin(ang) * sign_ref[...]


def _rope_tables(pos):
    t = pos.shape[0]
    half = HEAD_DIM_B // 2
    inv_freq = ROPE_THETA ** (-jnp.arange(half, dtype=F32) / half)
    invf = jnp.tile(inv_freq, LANES // half)[None, :]
    lane = jnp.arange(LANES)
    sign = jnp.where(lane % HEAD_DIM_B < half, -1.0, 1.0).astype(F32)[None, :]
    posb = jnp.broadcast_to(pos.astype(F32)[:, None], (t, LANES))
    return pl.pallas_call(
        _rope_table_kernel,
        out_shape=(jax.ShapeDtypeStruct((t, LANES), F32),) * 2,
        name="rope_tables",
    )(posb, invf, sign)


def _even_inproj_kernel(*refs, conv_tiles):
    x_ref, g_ref, w_ref, cos_ref, sin_ref = refs[:5]
    if conv_tiles is None:
        u_ref, ga_ref, q_ref, k_ref, v_ref, gb_ref = refs[5:]
        kt_ref = vt_ref = None
    else:
        buf_ref, cw_ref, cb_ref, lg_ref, lb_ref = refs[5:10]
        ya_ref, tail_ref, q_ref, k_ref, v_ref, gb_ref, kt_ref, vt_ref = refs[10:18]
        (ext_ref,) = refs[18:]
        if conv_tiles > 1:
            @pl.when(pl.program_id(0) == 0)
            def _():
                ext_ref[...] = jnp.zeros(ext_ref.shape, F32)
    x = x_ref[...]
    ms = jnp.mean(x * x, axis=-1, keepdims=True)
    h = (x * lax.rsqrt(ms + NORM_EPS) * g_ref[...]).astype(BF16)

    def seg(j):
        return jnp.dot(h, w_ref[:, j * D_A:(j + 1) * D_A], preferred_element_type=F32)

    cos = cos_ref[...]
    sin = sin_ref[...]
    lane = lax.broadcasted_iota(jnp.int32, cos.shape, 1)
    first_half = (lane % HEAD_DIM_B) < (HEAD_DIM_B // 2)

    def emit(val, out_ref, rotate, t_ref, pairs=range(N_PAIRS)):
        for c in pairs:
            xc = val[:, c * LANES:(c + 1) * LANES]
            if rotate:
                partner = jnp.where(first_half,
                                    pltpu.roll(xc, LANES - HEAD_DIM_B // 2, 1),
                                    pltpu.roll(xc, HEAD_DIM_B // 2, 1))
                xc = xc * cos + partner * sin
            out_ref[:, c * LANES:(c + 1) * LANES] = xc
            if t_ref is not None:
                xt = xc.T
                t_ref[0, 2 * c] = xt[0:HEAD_DIM_B]
                t_ref[0, 2 * c + 1] = xt[HEAD_DIM_B:LANES]
        return xc[0:SUBLANES]

    def emit_gate(val, cols):
        g = _silu(val[:, cols])
        gb_ref[:, cols] = g.astype(BF16)
        return g[0:SUBLANES, 0:LANES]

    def glu_and_gate():
        return seg(0) * jax.nn.sigmoid(seg(1)), _silu(seg(2))

    if conv_tiles is None:
        u, gate = glu_and_gate()
        u_ref[...] = u
        ga_ref[...] = gate.astype(BF16)
        emit(seg(3), q_ref, True, None)
        emit(seg(4), k_ref, True, kt_ref)
        emit(seg(5), v_ref, False, vt_ref)
        emit_gate(seg(6), slice(0, D_B))
    else:
        u, gate = glu_and_gate()
        vals = {}

        def piece(j, half):
            def run():
                if j not in vals:
                    vals[j] = seg(j)
                pairs = range(half * N_PAIRS // 2, (half + 1) * N_PAIRS // 2)
                if j == 3:
                    return emit(vals[j], q_ref, True, None, pairs)
                if j == 4:
                    return emit(vals[j], k_ref, True, kt_ref, pairs)
                if j == 5:
                    return emit(vals[j], v_ref, False, vt_ref, pairs)
                return emit_gate(vals[j], slice(half * D_B // 2, (half + 1) * D_B // 2))
            return run

        tc = x.shape[0]
        ya = _conv_tile(pl.program_id(0) % conv_tiles, u, gate, buf_ref, cw_ref, cb_ref, lg_ref, lb_ref,
                        tail_ref, ext_ref, tc=tc, t_valid=tc, nt=conv_tiles,
                        between=[piece(j, half) for j in (3, 4, 5, 6) for half in (0, 1)])
        ya_ref[...] = ya.astype(BF16)


def _even_inproj(x2d, pre_g, w_bf16, cos_t, sin_t, tm, conv=None):
    rows = x2d.shape[0]
    nt_tab = cos_t.shape[0] // tm
    row_spec = pl.BlockSpec((tm, D_MODEL), lambda i: (i, 0))
    tab_spec = pl.BlockSpec((tm, LANES), lambda i: (i % nt_tab, 0))
    f32_out = jax.ShapeDtypeStruct((rows, D_A), F32)
    bf_out = jax.ShapeDtypeStruct((rows, D_A), BF16)
    in_specs = [row_spec, _const_spec((1, D_MODEL)),
                pl.BlockSpec(w_bf16.shape, lambda i: (0, 0), pipeline_mode=pl.Buffered(1)),
                tab_spec, tab_spec]
    operands = [x2d, pre_g[None, :], w_bf16, cos_t, sin_t]
    if conv is None:
        conv_tiles = None
        out_specs = [row_spec] * 6
        out_shape = (f32_out, bf_out, f32_out, f32_out, f32_out, bf_out)
        scratch = []
    else:
        seq, hist, conv_w, conv_b, ln_g, ln_b = conv
        n = rows // seq
        conv_tiles = seq // tm
        halo = pl.BlockSpec((1, HALO, D_A), lambda i: (i // conv_tiles, 0, 0))
        t_spec = pl.BlockSpec((1, N_HEADS_B, HEAD_DIM_B, tm), lambda i: (i // conv_tiles, 0, 0, i % conv_tiles))
        t_out = jax.ShapeDtypeStruct((n, N_HEADS_B, HEAD_DIM_B, seq), F32)
        in_specs += [halo, _const_spec((CONV_WIDTH, D_A))] + [_const_spec((1, D_A))] * 3
        operands += [hist, conv_w, conv_b[None, :], ln_g[None, :], ln_b[None, :]]
        out_specs = [row_spec, halo] + [row_spec] * 4 + [t_spec] * 2
        out_shape = (bf_out, jax.ShapeDtypeStruct((n, HALO, D_A), F32),
                     f32_out, f32_out, f32_out, bf_out, t_out, t_out)
        scratch = _conv_scratch(tm)
    return pl.pallas_call(
        functools.partial(_even_inproj_kernel, conv_tiles=conv_tiles),
        grid=(rows // tm,),
        in_specs=in_specs,
        out_specs=out_specs,
        out_shape=out_shape,
        scratch_shapes=scratch,
        compiler_params=_params("arbitrary"),
        name="even_inproj",
    )(*operands)


def _conv_tile(t, u, gate, buf_ref, w_ref, cb_ref, lg_ref, lb_ref, tail_ref, ext_ref,
               *, tc, t_valid, nt, between=()):
    between = list(between)
    n_chunks = D_A // LANES
    if nt > 1:
        ext_ref[0:HALO, :] = jnp.where(t == 0, buf_ref[0], ext_ref[tc:tc + HALO, :])
    else:
        ext_ref[0:HALO, :] = buf_ref[0]
    ext_ref[HALO:HALO + tc, :] = u
    ext_ref[HALO + tc:HALO + tc + SUBLANES, :] = jnp.zeros((SUBLANES, D_A), F32)
    first = HALO - (CONV_WIDTH - 1)
    chunks = []
    anchor = None
    for c in range(n_chunks):
        for j, work in enumerate(between):
            if j * n_chunks // len(between) == c:
                anchor = work()
        cols = slice(c * LANES, (c + 1) * LANES)
        acc = jnp.zeros((tc, LANES), F32)
        if anchor is not None:
            bits = lax.shift_right_logical(lax.bitcast_convert_type(anchor, jnp.uint32), jnp.uint32(32))
            acc = acc + jnp.tile(lax.bitcast_convert_type(bits, F32), (tc // SUBLANES, 1))
        for rho in range(SUBLANES):
            part = None
            for k in range(CONV_WIDTH):
                if (first + k) % SUBLANES == rho:
                    base = first + k - rho
                    term = ext_ref[base:base + tc + SUBLANES, cols] * w_ref[k:k + 1, cols]
                    part = term if part is None else part + term
            if part is None:
                continue
            acc = acc + part[rho:rho + tc]
        chunks.append(acc + cb_ref[:, cols])
    y = jnp.concatenate(chunks, axis=1)
    yc = y - jnp.mean(y, axis=-1, keepdims=True)
    yn = yc * lax.rsqrt(jnp.mean(yc * yc, axis=-1, keepdims=True) + NORM_EPS)
    yn = yn * lg_ref[...] + lb_ref[...]
    tail_ref[0] = ext_ref[t_valid:t_valid + HALO, :]
    return _silu(yn) * gate


def _conv_scratch(tc):
    return [pltpu.VMEM((HALO + tc + SUBLANES, D_A), F32)]


def _conv_kernel(u_ref, ga_ref, buf_ref, w_ref, cb_ref, lg_ref, lb_ref,
                 ya_ref, tail_ref, ext_ref, *, tc, t_valid, nt):
    if nt > 1:
        @pl.when((pl.program_id(0) == 0) & (pl.program_id(1) == 0))
        def _():
            ext_ref[...] = jnp.zeros(ext_ref.shape, F32)
    ya = _conv_tile(pl.program_id(1), u_ref[0], ga_ref[0].astype(F32), buf_ref, w_ref, cb_ref, lg_ref, lb_ref,
                    tail_ref, ext_ref, tc=tc, t_valid=t_valid, nt=nt)
    ya_ref[0] = ya.astype(BF16)


def _conv_module(u, ga, buf, conv_w, conv_b, ln_g, ln_b, tc, t_valid):
    n, t, _ = u.shape
    nt = t // tc
    blk = pl.BlockSpec((1, tc, D_A), lambda b, i: (b, i, 0))
    halo = pl.BlockSpec((1, HALO, D_A), lambda b, i: (b, 0, 0))
    last_valid = t_valid - (nt - 1) * tc
    return pl.pallas_call(
        functools.partial(_conv_kernel, tc=tc, t_valid=last_valid, nt=nt),
        grid=(n, nt),
        in_specs=[blk, blk, halo, _const_spec((CONV_WIDTH, D_A)),
                  _const_spec((1, D_A)), _const_spec((1, D_A)), _const_spec((1, D_A))],
        out_specs=[blk, halo],
        out_shape=(jax.ShapeDtypeStruct((n, t, D_A), BF16),
                   jax.ShapeDtypeStruct((n, HALO, D_A), F32)),
        scratch_shapes=_conv_scratch(tc),
        compiler_params=_params("arbitrary", "arbitrary"),
        name="conv_module",
    )(u, ga, buf, conv_w, conv_b[None, :], ln_g[None, :], ln_b[None, :])


def _qk(q, k):
    return lax.dot_general(q, k, (((1,), (1,)), ((), ())), preferred_element_type=F32)


def _attn_prompt_kernel(q_ref, k_ref, v_ref, gb_ref, out_ref, q_s, k_s, v_s, o_s, m_s, l_s, *, seq):
    nblk = seq // SPAN
    scale = HEAD_DIM_B ** -0.5 * LOG2_E
    lo = lax.broadcasted_iota(jnp.int32, (SPAN, LANES), 1) < HEAD_DIM_B

    def rows(c, blk):
        dil = DSWA_CONFIGS[c][1]
        per_seq = nblk // dil
        return dil * SPAN * (blk % per_seq) + blk // per_seq, dil

    def take(ref, start, stride):
        if stride == 1:
            return ref[0, start:start + SPAN, :]
        return ref[0, pl.ds(start, SPAN, stride=stride), :]

    for c in range(len(DSWA_CONFIGS)):
        for blk in range(nblk):
            start, stride = rows(c, blk)
            qb = take(q_ref, start, stride) * scale
            q_s[c, blk, 0:SPAN, :] = jnp.where(lo, qb, 0.0).astype(BF16)
            q_s[c, blk, SPAN:2 * SPAN, :] = jnp.where(lo, 0.0, qb).astype(BF16)
            k_s[c, blk] = take(k_ref, start, stride).astype(BF16)
            vb = take(v_ref, start, stride)
            v_s[c, blk, 0] = jnp.where(lo, vb, 1.0).astype(BF16)
            v_s[c, blk, 1] = jnp.where(lo, 1.0, vb).astype(BF16)

    row = lax.broadcasted_iota(jnp.int32, (2 * SPAN, SPAN), 0) % SPAN
    col = lax.broadcasted_iota(jnp.int32, (2 * SPAN, SPAN), 1)
    upper = col >= row
    diag = col == row
    lower = col < row

    def has_prev(c, blk):
        return blk % (nblk // DSWA_CONFIGS[c][1]) > 0

    def scores(c, blk):
        q2 = q_s[c, blk]
        s_own = _qk(q2, k_s[c, blk])
        s_far = _qk(q2, k_s[c, blk - 1]) if has_prev(c, blk) else NEG
        return s_own, s_far

    def probabilities(s_own, s_far):
        s_self = jnp.sum(jnp.where(diag, s_own, 0.0), axis=-1, keepdims=True)
        s = jnp.where(upper, s_far, s_own)
        m = jnp.maximum(jnp.max(s, axis=-1, keepdims=True), s_self)
        p = jnp.exp2(s - m)
        p_self = jnp.exp2(s_self - m)
        p_own = jnp.where(lower, p, jnp.where(diag, p_self, 0.0)).astype(BF16)
        p_far = jnp.where(upper, p, 0.0).astype(BF16)
        return p_own, p_far, m

    def finish(c, blk, p_own, p_far, m):
        o = []
        for h in range(2):
            ph = slice(h * SPAN, (h + 1) * SPAN)
            o_h = jnp.dot(p_own[ph], v_s[c, blk, h], preferred_element_type=F32)
            if has_prev(c, blk):
                o_h = o_h + jnp.dot(p_far[ph], v_s[c, blk - 1, h], preferred_element_type=F32)
            o.append(o_h)
        m = jnp.broadcast_to(m, (2 * SPAN, LANES))
        start, stride = rows(c, blk)
        dst = pl.ds(start, SPAN, stride=stride) if stride > 1 else pl.ds(start, SPAN)
        o_s[c, dst, :] = jnp.where(lo, o[0], o[1])
        m_s[c, dst, :] = jnp.where(lo, m[0:SPAN], m[SPAN:2 * SPAN])
        l_s[c, dst, :] = jnp.where(lo, o[1], o[0])

    todo = [(c, blk) for c in range(len(DSWA_CONFIGS)) for blk in range(nblk)]
    groups = [todo[i:i + ATTN_GROUP] for i in range(0, len(todo), ATTN_GROUP)]
    pending = [scores(*cb) for cb in groups[0]]
    for gi, group in enumerate(groups):
        ahead = [scores(*cb) for cb in groups[gi + 1]] if gi + 1 < len(groups) else []
        probs = [probabilities(*sc) for sc in pending]
        for cb, pr in zip(group, probs):
            finish(*cb, *pr)
        pending = ahead

    ma, mb, mc = m_s[0], m_s[1], m_s[2]
    mx = jnp.maximum(jnp.maximum(ma, mb), mc)
    ea, eb, ec = jnp.exp2(ma - mx), jnp.exp2(mb - mx), jnp.exp2(mc - mx)
    la, lb, lc = (pltpu.roll(l_s[c], HEAD_DIM_B, 1) for c in range(3))
    merged = (ea * o_s[0] + eb * o_s[1] + ec * o_s[2]) / (ea * la + eb * lb + ec * lc)
    out_ref[0] = (merged * gb_ref[0].astype(F32)).astype(BF16)


def _attn_prompt(q, k, v, gb):
    n, seq, _ = q.shape
    nblk = seq // SPAN
    assert all(w // d == SPAN and nblk % d == 0 for w, d in DSWA_CONFIGS)
    ncfg = len(DSWA_CONFIGS)
    blk = pl.BlockSpec((1, seq, LANES), lambda b, h: (b, 0, h))
    return pl.pallas_call(
        functools.partial(_attn_prompt_kernel, seq=seq),
        grid=(n, N_PAIRS),
        in_specs=[blk] * 4,
        out_specs=blk,
        out_shape=jax.ShapeDtypeStruct((n, seq, D_B), BF16),
        scratch_shapes=[pltpu.VMEM((ncfg, nblk, 2 * SPAN, LANES), BF16),
                        pltpu.VMEM((ncfg, nblk, SPAN, LANES), BF16),
                        pltpu.VMEM((ncfg, nblk, 2, SPAN, LANES), BF16),
                        pltpu.VMEM((ncfg, seq, LANES), F32),
                        pltpu.VMEM((ncfg, seq, LANES), F32),
                        pltpu.VMEM((ncfg, seq, LANES), F32)],
        compiler_params=_params("arbitrary", "arbitrary"),
        name="attn_prompt",
    )(q, k, v, gb)


def _attn_sample_kernel(q_ref, kn_ref, vn_ref, kt_ref, vt_ref, gb_ref, out_ref, *, t_new, n_past, heads):
    nq = SAMPLE_PAD
    width = heads * HEAD_DIM_B
    scale = HEAD_DIM_B ** -0.5
    q_all = q_ref[0] * scale
    k_new = kn_ref[0]
    v_new = vn_ref[0]

    lane = lax.broadcasted_iota(jnp.int32, (nq, width), 1)
    qm = jnp.concatenate([jnp.where(lane // HEAD_DIM_B == h, q_all, 0.0) for h in range(heads)], axis=0)
    s_new = _qk(qm.astype(BF16), k_new.astype(BF16))
    t_row = lax.broadcasted_iota(jnp.int32, (nq, nq), 0)
    t_col = lax.broadcasted_iota(jnp.int32, (nq, nq), 1)

    dist = (n_past + lax.broadcasted_iota(jnp.int32, (nq, n_past), 0)
            - lax.broadcasted_iota(jnp.int32, (nq, n_past), 1))
    far = 1 << 30
    masks, starts = [], []
    for window, dil in DSWA_CONFIGS:
        start = n_past - window
        d = dist[:, start:]
        masks.append(jnp.where((d & (dil - 1)) == 0, d, far) <= window)
        starts.append(start)

    new_ok = [jnp.where(((t_row - t_col) & (dil - 1)) == 0, t_row - t_col, -1) >= 0 for _, dil in DSWA_CONFIGS]
    ncfg = len(DSWA_CONFIGS)
    cols = [slice(h * HEAD_DIM_B, (h + 1) * HEAD_DIM_B) for h in range(heads)]

    s_buf = [jnp.dot(q_all[:, cols[h]].astype(BF16), kt_ref[0, h].astype(BF16), preferred_element_type=F32)
             for h in range(heads)]
    stats = []
    for h in range(heads):
        sn = s_new[h * nq:(h + 1) * nq, :]
        per_cfg = []
        for c in range(ncfg):
            sc = jnp.where(masks[c], s_buf[h][:, starts[c]:], NEG)
            snm = jnp.where(new_ok[c], sn, NEG)
            m = jnp.maximum(jnp.max(sc, axis=-1, keepdims=True), jnp.max(snm, axis=-1, keepdims=True))
            p = jnp.exp(sc - m)
            pn = jnp.exp(snm - m)
            l = jnp.sum(p, axis=-1, keepdims=True) + jnp.sum(pn, axis=-1, keepdims=True)
            per_cfg.append((p.astype(BF16), pn, l, m + jnp.log(l)))
        stats.append(per_cfg)
    outs = []
    for h in range(heads):
        vt = vt_ref[0, h].astype(BF16)
        vnh = v_new[:, cols[h]]
        acc = []
        for c in range(ncfg):
            p, pn, l, _ = stats[h][c]
            o = _qk(p, vt[:, starts[c]:])
            for tn in range(t_new):
                o = o + pn[:, tn:tn + 1] * vnh[tn:tn + 1, :]
            acc.append(o / l)
        den = [stats[h][c][3] for c in range(ncfg)]
        mx = jnp.maximum(jnp.maximum(den[0], den[1]), den[2])
        e = [jnp.exp(d - mx) for d in den]
        outs.append((e[0] * acc[0] + e[1] * acc[1] + e[2] * acc[2]) / (e[0] + e[1] + e[2]))
    merged = jnp.concatenate(outs, axis=1)
    out_ref[0] = (merged * gb_ref[0].astype(F32)).astype(BF16)


def _attn_sample(q, k_new, v_new, kt_cache, vt_cache, gb, t_new, heads=8):
    n, _, _, n_past = kt_cache.shape
    assert n_past >= DSWA_CONFIGS[2][0] and t_new <= min(d for _, d in DSWA_CONFIGS[1:]) and t_new <= SAMPLE_PAD
    width = heads * HEAD_DIM_B
    new_blk = pl.BlockSpec((1, SAMPLE_PAD, width), lambda b, h: (b, 0, h))
    cache_blk = pl.BlockSpec((1, heads, HEAD_DIM_B, n_past), lambda b, h: (b, h, 0, 0))
    return pl.pallas_call(
        functools.partial(_attn_sample_kernel, t_new=t_new, n_past=n_past, heads=heads),
        grid=(n, N_HEADS_B // heads),
        in_specs=[new_blk] * 3 + [cache_blk] * 2 + [new_blk],
        out_specs=new_blk,
        out_shape=jax.ShapeDtypeStruct((n, SAMPLE_PAD, D_B), BF16),
        compiler_params=_params("arbitrary", "arbitrary"),
        name="attn_sample",
    )(q, k_new, v_new, kt_cache, vt_cache, gb)


def _outproj_kernel(*refs, n_in):
    ins, (w_ref, g_ref, x_ref, out_ref) = refs[:n_in], refs[n_in:]
    y = None
    off = 0
    for r in ins:
        width = r.shape[1]
        part = jnp.dot(r[...], w_ref[off:off + width, :], preferred_element_type=F32)
        y = part if y is None else y + part
        off += width
    yn = y * lax.rsqrt(jnp.mean(y * y, axis=-1, keepdims=True) + NORM_EPS) * g_ref[...]
    out_ref[...] = x_ref[...] + yn


def _outproj(parts, w_bf16, post_g, x2d, tm):
    rows = x2d.shape[0]
    row_spec = pl.BlockSpec((tm, D_MODEL), lambda i: (i, 0))
    return pl.pallas_call(
        functools.partial(_outproj_kernel, n_in=len(parts)),
        grid=(rows // tm,),
        in_specs=[pl.BlockSpec((tm, p.shape[1]), lambda i: (i, 0)) for p in parts]
        + [pl.BlockSpec(w_bf16.shape, lambda i: (0, 0), pipeline_mode=pl.Buffered(1)),
           _const_spec((1, D_MODEL)), row_spec],
        out_specs=row_spec,
        out_shape=jax.ShapeDtypeStruct((rows, D_MODEL), F32),
        compiler_params=_params("arbitrary"),
        name="outproj",
    )(*parts, w_bf16, post_g[None, :], x2d)


def _odd_inproj_kernel(x_ref, g_ref, w_ref, lbp_ref, q_ref, lf_ref, k_ref, v_ref, gt_ref, *, layer):
    x = x_ref[...]
    ms = jnp.mean(x * x, axis=-1, keepdims=True)
    h = (x * lax.rsqrt(ms + NORM_EPS) * g_ref[...]).astype(BF16)

    def seg(j):
        return jnp.dot(h, w_ref[:, j * D_C:(j + 1) * D_C], preferred_element_type=F32)

    lbp = lbp_ref[...]
    e = jnp.exp(lbp - jnp.max(lbp, axis=0, keepdims=True))
    sm = e / jnp.sum(e, axis=0, keepdims=True)
    lb = jnp.sum(sm[1:layer + 1], axis=0, keepdims=True)

    q_ref[...] = _silu(seg(0)).astype(BF16)
    f = lb + (1.0 - lb) * jax.nn.sigmoid(seg(1))
    lf_ref[...] = jnp.log(f)
    k_ref[...] = (1.0 - f).astype(BF16)
    v_ref[...] = seg(2).astype(BF16)
    gt_ref[...] = _silu(seg(3)).astype(BF16)


def _odd_inproj(x2d, pre_g, w_bf16, hgrn_lb, layer, tm):
    rows = x2d.shape[0]
    row_spec = pl.BlockSpec((tm, D_MODEL), lambda i: (i, 0))
    wide = pl.BlockSpec((tm, D_C), lambda i: (i, 0))
    bf = jax.ShapeDtypeStruct((rows, D_C), BF16)
    return pl.pallas_call(
        functools.partial(_odd_inproj_kernel, layer=layer),
        grid=(rows // tm,),
        in_specs=[row_spec, _const_spec((1, D_MODEL)),
                  pl.BlockSpec(w_bf16.shape, lambda i: (0, 0), pipeline_mode=pl.Buffered(1)),
                  _const_spec(hgrn_lb.shape)],
        out_specs=[wide] * 5,
        out_shape=(bf, jax.ShapeDtypeStruct((rows, D_C), F32), bf, bf, bf),
        compiler_params=_params("arbitrary"),
        name="odd_inproj",
    )(x2d, pre_g[None, :], w_bf16, hgrn_lb)


DIAG = 4
SHIFT_PAD = 8

def _hgrn_kernel(q_ref, lf_ref, k_ref, v_ref, gt_ref, gn_ref, s0_ref,
                 o_ref, sfin_ref, st_s, bp_s, kp_s, vp_s, tri_s, split_s, *, tile, t_valid, nt, hb):
    t = pl.program_id(2)

    @pl.when(t == 0)
    def _():
        for h in range(hb):
            st_s[h] = s0_ref[0, h].T

    @pl.when((pl.program_id(0) == 0) & (pl.program_id(1) == 0) & (t == 0))
    def _():
        rr = lax.broadcasted_iota(jnp.int32, (tile, tile), 0)
        cc = lax.broadcasted_iota(jnp.int32, (tile, tile), 1)
        tri_s[...] = jnp.where(rr >= cc, 1.0, 0.0).astype(BF16)
        split_s[...] = 31 - lax.clz(jnp.where(rr > cc, rr ^ cc, 0))

    row = lax.broadcasted_iota(jnp.int32, (tile, LANES), 0)
    tri = tri_s[...]
    split_bit = split_s[...]
    zpad = jnp.zeros((SHIFT_PAD, LANES), F32)
    same_block = [(row % DIAG) >= j for j in range(DIAG)]
    levels = []
    m = DIAG
    while 2 * m <= tile:
        levels.append(m)
        m *= 2

    heads = range(hb)
    cols = [slice(h * LANES, (h + 1) * LANES) for h in heads]
    qf, kf, vb, pieces = [], [], [], []
    for h in heads:
        lf = lf_ref[0, :, cols[h]] * LOG2_E
        k_h = k_ref[0, :, cols[h]].astype(F32)
        if t_valid < tile:
            lf = jnp.where(row < t_valid, lf, 0.0)
            k_h = jnp.where(row < t_valid, k_h, 0.0)
        kf.append(k_h)
        qf.append(q_ref[0, :, cols[h]].astype(F32))
        vb.append(v_ref[0, :, cols[h]])
        p1 = lf.astype(BF16)
        r1 = lf - p1.astype(F32)
        p2 = r1.astype(BF16)
        pieces.append((p1, p2, (r1 - p2.astype(F32)).astype(BF16)))

    b = []
    for h in heads:
        b_h = (jnp.dot(tri, pieces[h][0], preferred_element_type=F32)
               + jnp.dot(tri, pieces[h][1], preferred_element_type=F32)
               + jnp.dot(tri, pieces[h][2], preferred_element_type=F32))
        b.append(b_h)
        bp_s[h, 0:SHIFT_PAD, :] = zpad
        kp_s[h, 0:SHIFT_PAD, :] = zpad
        vp_s[h, 0:SHIFT_PAD, :] = zpad
        bp_s[h, SHIFT_PAD:SHIFT_PAD + tile, :] = b_h
        kp_s[h, SHIFT_PAD:SHIFT_PAD + tile, :] = kf[h]
        vp_s[h, SHIFT_PAD:SHIFT_PAD + tile, :] = vb[h].astype(F32)

    attn = [jnp.zeros((tile, tile), F32) for _ in heads]
    for m in levels:
        nb = tile // (2 * m)
        for h in heads:
            if nb > 1:
                mid = bp_s[h, pl.ds(SHIFT_PAD + m - 1, nb, stride=2 * m), :]
                mid = jnp.broadcast_to(mid[:, None, :], (nb, 2 * m, LANES)).reshape(tile, LANES)
            else:
                mid = jnp.broadcast_to(bp_s[h, SHIFT_PAD + m - 1:SHIFT_PAD + m, :], (tile, LANES))
            decay = jnp.exp2(-jnp.abs(b[h] - mid))
            attn[h] = jnp.where(split_bit == m.bit_length() - 1,
                                _qk((qf[h] * decay).astype(BF16), (kf[h] * decay).astype(BF16)), attn[h])
    o = [jnp.dot(attn[h].astype(BF16), vb[h], preferred_element_type=F32) for h in heads]

    for h in heads:
        vf = vp_s[h, SHIFT_PAD:SHIFT_PAD + tile, :]
        o[h] = o[h] + jnp.sum(qf[h] * kf[h], axis=-1, keepdims=True) * vf
        for j in range(1, DIAG):
            bj = bp_s[h, SHIFT_PAD - j:SHIFT_PAD - j + tile, :]
            kj = kp_s[h, SHIFT_PAD - j:SHIFT_PAD - j + tile, :]
            vj = vp_s[h, SHIFT_PAD - j:SHIFT_PAD - j + tile, :]
            e = jnp.where(same_block[j], b[h] - bj, NEG)
            w = jnp.sum(qf[h] * kj * jnp.exp2(e), axis=-1, keepdims=True)
            o[h] = o[h] + w * vj

    st = [st_s[h] for h in heads]
    for h in heads:
        o[h] = o[h] + _qk((qf[h] * jnp.exp2(b[h])).astype(BF16), st[h].astype(BF16))
    for h in heads:
        b_last = bp_s[h, SHIFT_PAD + tile - 1:SHIFT_PAD + tile, :]
        kdec = (kf[h] * jnp.exp2(b_last - b[h])).astype(BF16)
        st_s[h] = jnp.exp2(b_last) * st[h] + lax.dot_general(
            vb[h], kdec, (((0,), (0,)), ((), ())), preferred_element_type=F32)

    for h in heads:
        on = o[h] * lax.rsqrt(jnp.mean(o[h] * o[h], axis=-1, keepdims=True) + NORM_EPS) * gn_ref[...]
        o_ref[0, :, cols[h]] = (on * gt_ref[0, :, cols[h]].astype(F32)).astype(BF16)

    @pl.when(t == nt - 1)
    def _():
        for h in range(hb):
            sfin_ref[0, h] = st_s[h].T


def _hgrn(q, lf, k, v, gt, g_norm, s0, tile, t_valid, hb):
    n, t, _ = q.shape
    nt = t // tile
    blk = pl.BlockSpec((1, tile, hb * LANES), lambda b, h, i: (b, i, h))
    st_blk = pl.BlockSpec((1, hb, HGRN_HEAD, HGRN_HEAD), lambda b, h, i: (b, h, 0, 0))
    pad = pltpu.VMEM((hb, SHIFT_PAD + tile, LANES), F32)
    return pl.pallas_call(
        functools.partial(_hgrn_kernel, tile=tile, t_valid=t_valid, nt=nt, hb=hb),
        grid=(n, N_HEADS_C // hb, nt),
        in_specs=[blk] * 5 + [_const_spec((1, LANES)), st_blk],
        out_specs=[blk, st_blk],
        out_shape=(jax.ShapeDtypeStruct((n, t, D_C), BF16),
                   jax.ShapeDtypeStruct((n, N_HEADS_C, HGRN_HEAD, HGRN_HEAD), F32)),
        scratch_shapes=[pltpu.VMEM((hb, HGRN_HEAD, HGRN_HEAD), F32),
                        pad, pad, pad,
                        pltpu.VMEM((tile, tile), BF16), pltpu.VMEM((tile, tile), jnp.int32)],
        compiler_params=_params("arbitrary", "arbitrary", "arbitrary"),
        name="hgrn",
    )(q, lf, k, v, gt, g_norm[None, :], s0)


def _tiles(n, t):
    rows = n * t
    tm = min(256, rows)
    tm_wide = min(512, rows)
    tc = min(256, t)
    th = min(128, t)
    return tm, tm_wide, tc, th, N_HEADS_C


def _trunk(x, pos, conv_hist, kv_cache, state0, t_valid, prm):
    n, t, _ = x.shape
    rows = n * t
    tm, tm_wide, tc, th, hb = _tiles(n, t)
    x2d = x.reshape(rows, D_MODEL)
    cos_t, sin_t = _rope_tables(pos)
    if cos_t.shape[0] < tm:
        cos_t = jnp.tile(cos_t, (tm // t, 1))
        sin_t = jnp.tile(sin_t, (tm // t, 1))
    shp = (n, t, D_A)
    conv_prm = (prm["conv_w"], prm["conv_b"], prm["ln_g"], prm["ln_b"])
    if kv_cache is None:
        assert t_valid == t and t % tm == 0
        ya, hist, q, k, v, gb, *kv_t = _even_inproj(x2d, prm["pre0"], prm["w_in_ab"], cos_t, sin_t, tm,
                                                    conv=(t, conv_hist) + conv_prm)
        yb = _attn_prompt(q.reshape(shp), k.reshape(shp), v.reshape(shp), gb.reshape(shp))
    else:
        kv_t = None
        u, ga, q, k, v, gb = _even_inproj(x2d, prm["pre0"], prm["w_in_ab"], cos_t, sin_t, tm)
        ya, hist = _conv_module(u.reshape(shp), ga.reshape(shp), conv_hist, *conv_prm, tc, t_valid)
        yb = _attn_sample(q.reshape(shp), k.reshape(shp), v.reshape(shp), kv_cache[0], kv_cache[1],
                          gb.reshape(shp), t_valid)
    x1 = _outproj([ya.reshape(rows, D_A), yb.reshape(rows, D_B)], prm["w_out_ab"], prm["post0"], x2d, tm_wide)
    qc, lf, kc, vc, gt = _odd_inproj(x1, prm["pre1"], prm["w_in_c"], prm["hgrn_lb"], 1, tm)
    wide = (n, t, D_C)
    oc, s_fin = _hgrn(qc.reshape(wide), lf.reshape(wide), kc.reshape(wide), vc.reshape(wide),
                      gt.reshape(wide), prm["gnorm"], state0, th, t_valid, hb)
    x2 = _outproj([oc.reshape(rows, D_C)], prm["w_out_c"], prm["post1"], x1, tm_wide)
    if kv_t is None:
        kv_t = (k.reshape(shp), v.reshape(shp))
    return x2.reshape(n, t, D_MODEL), hist, kv_t[0], kv_t[1], s_fin


def kernel(x_prompt, x_sample, cache_conv, cache_swa_k, cache_swa_v, state_hgrn, pre_norm, post_norm,
           w_in_ab, w_out_ab, conv_w, conv_b, conv_ln_g, conv_ln_b, w_in_c, w_out_c, hgrn_gnorm, hgrn_lb):
    n_p, seq_p, _ = x_prompt.shape
    n_s, seq_s, _ = x_sample.shape
    hist_rows = CONV_WIDTH - 1
    prm = dict(pre0=pre_norm[0], pre1=pre_norm[1], post0=post_norm[0], post1=post_norm[1],
               w_in_ab=w_in_ab[0].astype(BF16), w_out_ab=w_out_ab[0].astype(BF16),
               conv_w=conv_w[0], conv_b=conv_b[0], ln_g=conv_ln_g[0], ln_b=conv_ln_b[0],
               w_in_c=w_in_c[0].astype(BF16), w_out_c=w_out_c[0].astype(BF16),
               gnorm=hgrn_gnorm[0], hgrn_lb=hgrn_lb)

    pos_p = jnp.arange(seq_p, dtype=jnp.int32)
    yp, hist_p, k_p, v_p, s_p = _trunk(
        x_prompt, pos_p, jnp.zeros((n_p, HALO, D_A), F32), None,
        jnp.zeros((n_p, N_HEADS_C, HGRN_HEAD, HGRN_HEAD), F32), seq_p, prm)

    pad = SAMPLE_PAD - seq_s
    xs = jnp.pad(x_sample, ((0, 0), (0, pad), (0, 0)))
    pos_s = PAST_LEN + jnp.arange(SAMPLE_PAD, dtype=jnp.int32)
    hist_s = jnp.pad(cache_conv[0], ((0, 0), (HALO - hist_rows, 0), (0, 0)))
    kv = (jnp.transpose(cache_swa_k[0], (0, 2, 3, 1)), jnp.transpose(cache_swa_v[0], (0, 2, 3, 1)))
    ys, hist_s, k_s, v_s, s_s = _trunk(
        xs, pos_s, hist_s, kv, state_hgrn[0], seq_s, prm)

    heads = (N_HEADS_B, HEAD_DIM_B)
    return (yp, ys[:, :seq_s],
            hist_p[None, :, HALO - hist_rows:], hist_s[None, :, HALO - hist_rows:],
            jnp.transpose(k_p, (0, 3, 1, 2))[None], jnp.transpose(v_p, (0, 3, 1, 2))[None],
            k_s[:, :seq_s].reshape(1, n_s, seq_s, *heads), v_s[:, :seq_s].reshape(1, n_s, seq_s, *heads),
            s_p[None], s_s[None])
```

```python
import functools

import jax
import jax.numpy as jnp
from jax import lax
from jax.experimental import pallas as pl
from jax.experimental.pallas import tpu as pltpu

F32 = jnp.float32
BF16 = jnp.bfloat16

D_MODEL = 1024
D_A = 1024
D_B = 1024
CONV_WIDTH = 31
HALO = 32
HEAD_DIM_B = 64
N_HEADS_B = 16
LANES = 128
SUBLANES = 8
N_PAIRS = D_B // LANES
DSWA_CONFIGS = ((128, 1), (512, 4), (2048, 16))
SPAN = 128
ROPE_THETA = 10000.0
PAST_LEN = 16384
HGRN_HEAD = 128
N_HEADS_C = 16
D_C = N_HEADS_C * HGRN_HEAD
NORM_EPS = 1e-6
NEG = -1e30
LOG2_E = 1.4426950408889634
SAMPLE_PAD = 16
ATTN_GROUP = 4
VMEM_LIMIT = 56 * 1024 * 1024


def _silu(x):
    return x * jax.nn.sigmoid(x)


def _params(*sem):
    return pltpu.CompilerParams(dimension_semantics=sem, vmem_limit_bytes=VMEM_LIMIT)


def _const_spec(shape):
    return pl.BlockSpec(shape, lambda *_: (0,) * len(shape))


def _rope_table_kernel(pos_ref, invf_ref, sign_ref, cos_ref, sin_ref):
    ang = pos_ref[...] * invf_ref[...]
    cos_ref[...] = jnp.cos(ang)
    sin_ref[...] = jnp.sin(ang) * sign_ref[...]


def _rope_tables(pos):
    t = pos.shape[0]
    half = HEAD_DIM_B // 2
    inv_freq = ROPE_THETA ** (-jnp.arange(half, dtype=F32) / half)
    invf = jnp.tile(inv_freq, LANES // half)[None, :]
    lane = jnp.arange(LANES)
    sign = jnp.where(lane % HEAD_DIM_B < half, -1.0, 1.0).astype(F32)[None, :]
    posb = jnp.broadcast_to(pos.astype(F32)[:, None], (t, LANES))
    return pl.pallas_call(
        _rope_table_kernel,
        out_shape=(jax.ShapeDtypeStruct((t, LANES), F32),) * 2,
        name="rope_tables",
    )(posb, invf, sign)


def _even_inproj_kernel(*refs, conv_tiles):
    x_ref, g_ref, w_ref, cos_ref, sin_ref = refs[:5]
    if conv_tiles is None:
        u_ref, ga_ref, q_ref, k_ref, v_ref, gb_ref = refs[5:]
        kt_ref = vt_ref = None
    else:
        buf_ref, cw_ref, cb_ref, lg_ref, lb_ref = refs[5:10]
        ya_ref, tail_ref, q_ref, k_ref, v_ref, gb_ref, kt_ref, vt_ref = refs[10:18]
        (ext_ref,) = refs[18:]
        if conv_tiles > 1:
            @pl.when(pl.program_id(0) == 0)
            def _():
                ext_ref[...] = jnp.zeros(ext_ref.shape, F32)
    x = x_ref[...]
    ms = jnp.mean(x * x, axis=-1, keepdims=True)
    h = (x * lax.rsqrt(ms + NORM_EPS) * g_ref[...]).astype(BF16)

    def seg(j):
        return jnp.dot(h, w_ref[:, j * D_A:(j + 1) * D_A], preferred_element_type=F32)

    cos = cos_ref[...]
    sin = sin_ref[...]
    lane = lax.broadcasted_iota(jnp.int32, cos.shape, 1)
    first_half = (lane % HEAD_DIM_B) < (HEAD_DIM_B // 2)

    def emit(val, out_ref, rotate, t_ref, pairs=range(N_PAIRS)):
        for c in pairs:
            xc = val[:, c * LANES:(c + 1) * LANES]
            if rotate:
                partner = jnp.where(first_half,
                                    pltpu.roll(xc, LANES - HEAD_DIM_B // 2, 1),
                                    pltpu.roll(xc, HEAD_DIM_B // 2, 1))
                xc = xc * cos + partner * sin
            out_ref[:, c * LANES:(c + 1) * LANES] = xc
            if t_ref is not None:
                xt = xc.T
                t_ref[0, 2 * c] = xt[0:HEAD_DIM_B]
                t_ref[0, 2 * c + 1] = xt[HEAD_DIM_B:LANES]
        return xc[0:SUBLANES]

    def emit_gate(val, cols):
        g = _silu(val[:, cols])
        gb_ref[:, cols] = g.astype(BF16)
        return g[0:SUBLANES, 0:LANES]

    def glu_and_gate():
        return seg(0) * jax.nn.sigmoid(seg(1)), _silu(seg(2))

    if conv_tiles is None:
        u, gate = glu_and_gate()
        u_ref[...] = u
        ga_ref[...] = gate.astype(BF16)
        emit(seg(3), q_ref, True, None)
        emit(seg(4), k_ref, True, kt_ref)
        emit(seg(5), v_ref, False, vt_ref)
        emit_gate(seg(6), slice(0, D_B))
    else:
        u, gate = glu_and_gate()
        vals = {}

        def piece(j, half):
            def run():
                if j not in vals:
                    vals[j] = seg(j)
                pairs = range(half * N_PAIRS // 2, (half + 1) * N_PAIRS // 2)
                if j == 3:
                    return emit(vals[j], q_ref, True, None, pairs)
                if j == 4:
                    return emit(vals[j], k_ref, True, kt_ref, pairs)
                if j == 5:
                    return emit(vals[j], v_ref, False, vt_ref, pairs)
                return emit_gate(vals[j], slice(half * D_B // 2, (half + 1) * D_B // 2))
            return run

        tc = x.shape[0]
        ya = _conv_tile(pl.program_id(0) % conv_tiles, u, gate, buf_ref, cw_ref, cb_ref, lg_ref, lb_ref,
                        tail_ref, ext_ref, tc=tc, t_valid=tc, nt=conv_tiles,
                        between=[piece(j, half) for j in (3, 4, 5, 6) for half in (0, 1)])
        ya_ref[...] = ya.astype(BF16)


def _even_inproj(x2d, pre_g, w_bf16, cos_t, sin_t, tm, conv=None):
    rows = x2d.shape[0]
    nt_tab = cos_t.shape[0] // tm
    row_spec = pl.BlockSpec((tm, D_MODEL), lambda i: (i, 0))
    tab_spec = pl.BlockSpec((tm, LANES), lambda i: (i % nt_tab, 0))
    f32_out = jax.ShapeDtypeStruct((rows, D_A), F32)
    bf_out = jax.ShapeDtypeStruct((rows, D_A), BF16)
    in_specs = [row_spec, _const_spec((1, D_MODEL)),
                pl.BlockSpec(w_bf16.shape, lambda i: (0, 0), pipeline_mode=pl.Buffered(1)),
                tab_spec, tab_spec]
    operands = [x2d, pre_g[None, :], w_bf16, cos_t, sin_t]
    if conv is None:
        conv_tiles = None
        out_specs = [row_spec] * 6
        out_shape = (f32_out, bf_out, f32_out, f32_out, f32_out, bf_out)
        scratch = []
    else:
        seq, hist, conv_w, conv_b, ln_g, ln_b = conv
        n = rows // seq
        conv_tiles = seq // tm
        halo = pl.BlockSpec((1, HALO, D_A), lambda i: (i // conv_tiles, 0, 0))
        t_spec = pl.BlockSpec((1, N_HEADS_B, HEAD_DIM_B, tm), lambda i: (i // conv_tiles, 0, 0, i % conv_tiles))
        t_out = jax.ShapeDtypeStruct((n, N_HEADS_B, HEAD_DIM_B, seq), F32)
        in_specs += [halo, _const_spec((CONV_WIDTH, D_A))] + [_const_spec((1, D_A))] * 3
        operands += [hist, conv_w, conv_b[None, :], ln_g[None, :], ln_b[None, :]]
        out_specs = [row_spec, halo] + [row_spec] * 4 + [t_spec] * 2
        out_shape = (bf_out, jax.ShapeDtypeStruct((n, HALO, D_A), F32),
                     f32_out, f32_out, f32_out, bf_out, t_out, t_out)
        scratch = _conv_scratch(tm)
    return pl.pallas_call(
        functools.partial(_even_inproj_kernel, conv_tiles=conv_tiles),
        grid=(rows // tm,),
        in_specs=in_specs,
        out_specs=out_specs,
        out_shape=out_shape,
        scratch_shapes=scratch,
        compiler_params=_params("arbitrary"),
        name="even_inproj",
    )(*operands)


def _conv_tile(t, u, gate, buf_ref, w_ref, cb_ref, lg_ref, lb_ref, tail_ref, ext_ref,
               *, tc, t_valid, nt, between=()):
    between = list(between)
    n_chunks = D_A // LANES
    if nt > 1:
        ext_ref[0:HALO, :] = jnp.where(t == 0, buf_ref[0], ext_ref[tc:tc + HALO, :])
    else:
        ext_ref[0:HALO, :] = buf_ref[0]
    ext_ref[HALO:HALO + tc, :] = u
    ext_ref[HALO + tc:HALO + tc + SUBLANES, :] = jnp.zeros((SUBLANES, D_A), F32)
    first = HALO - (CONV_WIDTH - 1)
    chunks = []
    anchor = None
    for c in range(n_chunks):
        for j, work in enumerate(between):
            if j * n_chunks // len(between) == c:
                anchor = work()
        cols = slice(c * LANES, (c + 1) * LANES)
        acc = jnp.zeros((tc, LANES), F32)
        if anchor is not None:
            bits = lax.shift_right_logical(lax.bitcast_convert_type(anchor, jnp.uint32), jnp.uint32(32))
            acc = acc + jnp.tile(lax.bitcast_convert_type(bits, F32), (tc // SUBLANES, 1))
        for rho in range(SUBLANES):
            part = None
            for k in range(CONV_WIDTH):
                if (first + k) % SUBLANES == rho:
                    base = first + k - rho
                    term = ext_ref[base:base + tc + SUBLANES, cols] * w_ref[k:k + 1, cols]
                    part = term if part is None else part + term
            if part is None:
                continue
            acc = acc + part[rho:rho + tc]
        chunks.append(acc + cb_ref[:, cols])
    y = jnp.concatenate(chunks, axis=1)
    yc = y - jnp.mean(y, axis=-1, keepdims=True)
    yn = yc * lax.rsqrt(jnp.mean(yc * yc, axis=-1, keepdims=True) + NORM_EPS)
    yn = yn * lg_ref[...] + lb_ref[...]
    tail_ref[0] = ext_ref[t_valid:t_valid + HALO, :]
    return _silu(yn) * gate


def _conv_scratch(tc):
    return [pltpu.VMEM((HALO + tc + SUBLANES, D_A), F32)]


def _conv_kernel(u_ref, ga_ref, buf_ref, w_ref, cb_ref, lg_ref, lb_ref,
                 ya_ref, tail_ref, ext_ref, *, tc, t_valid, nt):
    if nt > 1:
        @pl.when((pl.program_id(0) == 0) & (pl.program_id(1) == 0))
        def _():
            ext_ref[...] = jnp.zeros(ext_ref.shape, F32)
    ya = _conv_tile(pl.program_id(1), u_ref[0], ga_ref[0].astype(F32), buf_ref, w_ref, cb_ref, lg_ref, lb_ref,
                    tail_ref, ext_ref, tc=tc, t_valid=t_valid, nt=nt)
    ya_ref[0] = ya.astype(BF16)


def _conv_module(u, ga, buf, conv_w, conv_b, ln_g, ln_b, tc, t_valid):
    n, t, _ = u.shape
    nt = t // tc
    blk = pl.BlockSpec((1, tc, D_A), lambda b, i: (b, i, 0))
    halo = pl.BlockSpec((1, HALO, D_A), lambda b, i: (b, 0, 0))
    last_valid = t_valid - (nt - 1) * tc
    return pl.pallas_call(
        functools.partial(_conv_kernel, tc=tc, t_valid=last_valid, nt=nt),
        grid=(n, nt),
        in_specs=[blk, blk, halo, _const_spec((CONV_WIDTH, D_A)),
                  _const_spec((1, D_A)), _const_spec((1, D_A)), _const_spec((1, D_A))],
        out_specs=[blk, halo],
        out_shape=(jax.ShapeDtypeStruct((n, t, D_A), BF16),
                   jax.ShapeDtypeStruct((n, HALO, D_A), F32)),
        scratch_shapes=_conv_scratch(tc),
        compiler_params=_params("arbitrary", "arbitrary"),
        name="conv_module",
    )(u, ga, buf, conv_w, conv_b[None, :], ln_g[None, :], ln_b[None, :])


def _qk(q, k):
    return lax.dot_general(q, k, (((1,), (1,)), ((), ())), preferred_element_type=F32)


def _attn_prompt_kernel(q_ref, k_ref, v_ref, gb_ref, out_ref, q_s, k_s, v_s, o_s, m_s, l_s, *, seq):
    nblk = seq // SPAN
    scale = HEAD_DIM_B ** -0.5 * LOG2_E
    lo = lax.broadcasted_iota(jnp.int32, (SPAN, LANES), 1) < HEAD_DIM_B

    def rows(c, blk):
        dil = DSWA_CONFIGS[c][1]
        per_seq = nblk // dil
        return dil * SPAN * (blk % per_seq) + blk // per_seq, dil

    def take(ref, start, stride):
        if stride == 1:
            return ref[0, start:start + SPAN, :]
        return ref[0, pl.ds(start, SPAN, stride=stride), :]

    for c in range(len(DSWA_CONFIGS)):
        for blk in range(nblk):
            start, stride = rows(c, blk)
            qb = take(q_ref, start, stride) * scale
            q_s[c, blk, 0:SPAN, :] = jnp.where(lo, qb, 0.0).astype(BF16)
            q_s[c, blk, SPAN:2 * SPAN, :] = jnp.where(lo, 0.0, qb).astype(BF16)
            k_s[c, blk] = take(k_ref, start, stride).astype(BF16)
            vb = take(v_ref, start, stride)
            v_s[c, blk, 0] = jnp.where(lo, vb, 1.0).astype(BF16)
            v_s[c, blk, 1] = jnp.where(lo, 1.0, vb).astype(BF16)

    row = lax.broadcasted_iota(jnp.int32, (2 * SPAN, SPAN), 0) % SPAN
    col = lax.broadcasted_iota(jnp.int32, (2 * SPAN, SPAN), 1)
    upper = col >= row
    diag = col == row
    lower = col < row

    def has_prev(c, blk):
        return blk % (nblk // DSWA_CONFIGS[c][1]) > 0

    def scores(c, blk):
        q2 = q_s[c, blk]
        s_own = _qk(q2, k_s[c, blk])
        s_far = _qk(q2, k_s[c, blk - 1]) if has_prev(c, blk) else NEG
        return s_own, s_far

    def probabilities(s_own, s_far):
        s_self = jnp.sum(jnp.where(diag, s_own, 0.0), axis=-1, keepdims=True)
        s = jnp.where(upper, s_far, s_own)
        m = jnp.maximum(jnp.max(s, axis=-1, keepdims=True), s_self)
        p = jnp.exp2(s - m)
        p_self = jnp.exp2(s_self - m)
        p_own = jnp.where(lower, p, jnp.where(diag, p_self, 0.0)).astype(BF16)
        p_far = jnp.where(upper, p, 0.0).astype(BF16)
        return p_own, p_far, m

    def finish(c, blk, p_own, p_far, m):
        o = []
        for h in range(2):
            ph = slice(h * SPAN, (h + 1) * SPAN)
            o_h = jnp.dot(p_own[ph], v_s[c, blk, h], preferred_element_type=F32)
            if has_prev(c, blk):
                o_h = o_h + jnp.dot(p_far[ph], v_s[c, blk - 1, h], preferred_element_type=F32)
            o.append(o_h)
        m = jnp.broadcast_to(m, (2 * SPAN, LANES))
        start, stride = rows(c, blk)
        dst = pl.ds(start, SPAN, stride=stride) if stride > 1 else pl.ds(start, SPAN)
        o_s[c, dst, :] = jnp.where(lo, o[0], o[1])
        m_s[c, dst, :] = jnp.where(lo, m[0:SPAN], m[SPAN:2 * SPAN])
        l_s[c, dst, :] = jnp.where(lo, o[1], o[0])

    todo = [(c, blk) for c in range(len(DSWA_CONFIGS)) for blk in range(nblk)]
    groups = [todo[i:i + ATTN_GROUP] for i in range(0, len(todo), ATTN_GROUP)]
    pending = [scores(*cb) for cb in groups[0]]
    for gi, group in enumerate(groups):
        ahead = [scores(*cb) for cb in groups[gi + 1]] if gi + 1 < len(groups) else []
        probs = [probabilities(*sc) for sc in pending]
        for cb, pr in zip(group, probs):
            finish(*cb, *pr)
        pending = ahead

    ma, mb, mc = m_s[0], m_s[1], m_s[2]
    mx = jnp.maximum(jnp.maximum(ma, mb), mc)
    ea, eb, ec = jnp.exp2(ma - mx), jnp.exp2(mb - mx), jnp.exp2(mc - mx)
    la, lb, lc = (pltpu.roll(l_s[c], HEAD_DIM_B, 1) for c in range(3))
    merged = (ea * o_s[0] + eb * o_s[1] + ec * o_s[2]) / (ea * la + eb * lb + ec * lc)
    out_ref[0] = (merged * gb_ref[0].astype(F32)).astype(BF16)


def _attn_prompt(q, k, v, gb):
    n, seq, _ = q.shape
    nblk = seq // SPAN
    assert all(w // d == SPAN and nblk % d == 0 for w, d in DSWA_CONFIGS)
    ncfg = len(DSWA_CONFIGS)
    blk = pl.BlockSpec((1, seq, LANES), lambda b, h: (b, 0, h))
    return pl.pallas_call(
        functools.partial(_attn_prompt_kernel, seq=seq),
        grid=(n, N_PAIRS),
        in_specs=[blk] * 4,
        out_specs=blk,
        out_shape=jax.ShapeDtypeStruct((n, seq, D_B), BF16),
        scratch_shapes=[pltpu.VMEM((ncfg, nblk, 2 * SPAN, LANES), BF16),
                        pltpu.VMEM((ncfg, nblk, SPAN, LANES), BF16),
                        pltpu.VMEM((ncfg, nblk, 2, SPAN, LANES), BF16),
                        pltpu.VMEM((ncfg, seq, LANES), F32),
                        pltpu.VMEM((ncfg, seq, LANES), F32),
                        pltpu.VMEM((ncfg, seq, LANES), F32)],
        compiler_params=_params("arbitrary", "arbitrary"),
        name="attn_prompt",
    )(q, k, v, gb)


def _attn_sample_kernel(q_ref, kn_ref, vn_ref, kt_ref, vt_ref, gb_ref, out_ref, *, t_new, n_past, heads):
    nq = SAMPLE_PAD
    width = heads * HEAD_DIM_B
    scale = HEAD_DIM_B ** -0.5
    q_all = q_ref[0] * scale
    k_new = kn_ref[0]
    v_new = vn_ref[0]

    lane = lax.broadcasted_iota(jnp.int32, (nq, width), 1)
    qm = jnp.concatenate([jnp.where(lane // HEAD_DIM_B == h, q_all, 0.0) for h in range(heads)], axis=0)
    s_new = _qk(qm.astype(BF16), k_new.astype(BF16))
    t_row = lax.broadcasted_iota(jnp.int32, (nq, nq), 0)
    t_col = lax.broadcasted_iota(jnp.int32, (nq, nq), 1)

    dist = (n_past + lax.broadcasted_iota(jnp.int32, (nq, n_past), 0)
            - lax.broadcasted_iota(jnp.int32, (nq, n_past), 1))
    far = 1 << 30
    masks, starts = [], []
    for window, dil in DSWA_CONFIGS:
        start = n_past - window
        d = dist[:, start:]
        masks.append(jnp.where((d & (dil - 1)) == 0, d, far) <= window)
        starts.append(start)

    new_ok = [jnp.where(((t_row - t_col) & (dil - 1)) == 0, t_row - t_col, -1) >= 0 for _, dil in DSWA_CONFIGS]
    ncfg = len(DSWA_CONFIGS)
    cols = [slice(h * HEAD_DIM_B, (h + 1) * HEAD_DIM_B) for h in range(heads)]

    s_buf = [jnp.dot(q_all[:, cols[h]].astype(BF16), kt_ref[0, h].astype(BF16), preferred_element_type=F32)
             for h in range(heads)]
    stats = []
    for h in range(heads):
        sn = s_new[h * nq:(h + 1) * nq, :]
        per_cfg = []
        for c in range(ncfg):
            sc = jnp.where(masks[c], s_buf[h][:, starts[c]:], NEG)
            snm = jnp.where(new_ok[c], sn, NEG)
            m = jnp.maximum(jnp.max(sc, axis=-1, keepdims=True), jnp.max(snm, axis=-1, keepdims=True))
            p = jnp.exp(sc - m)
            pn = jnp.exp(snm - m)
            l = jnp.sum(p, axis=-1, keepdims=True) + jnp.sum(pn, axis=-1, keepdims=True)
            per_cfg.append((p.astype(BF16), pn, l, m + jnp.log(l)))
        stats.append(per_cfg)
    outs = []
    for h in range(heads):
        vt = vt_ref[0, h].astype(BF16)
        vnh = v_new[:, cols[h]]
        acc = []
        for c in range(ncfg):
            p, pn, l, _ = stats[h][c]
            o = _qk(p, vt[:, starts[c]:])
            for tn in range(t_new):
                o = o + pn[:, tn:tn + 1] * vnh[tn:tn + 1, :]
            acc.append(o / l)
        den = [stats[h][c][3] for c in range(ncfg)]
        mx = jnp.maximum(jnp.maximum(den[0], den[1]), den[2])
        e = [jnp.exp(d - mx) for d in den]
        outs.append((e[0] * acc[0] + e[1] * acc[1] + e[2] * acc[2]) / (e[0] + e[1] + e[2]))
    merged = jnp.concatenate(outs, axis=1)
    out_ref[0] = (merged * gb_ref[0].astype(F32)).astype(BF16)


def _attn_sample(q, k_new, v_new, kt_cache, vt_cache, gb, t_new, heads=8):
    n, _, _, n_past = kt_cache.shape
    assert n_past >= DSWA_CONFIGS[2][0] and t_new <= min(d for _, d in DSWA_CONFIGS[1:]) and t_new <= SAMPLE_PAD
    width = heads * HEAD_DIM_B
    new_blk = pl.BlockSpec((1, SAMPLE_PAD, width), lambda b, h: (b, 0, h))
    cache_blk = pl.BlockSpec((1, heads, HEAD_DIM_B, n_past), lambda b, h: (b, h, 0, 0))
    return pl.pallas_call(
        functools.partial(_attn_sample_kernel, t_new=t_new, n_past=n_past, heads=heads),
        grid=(n, N_HEADS_B // heads),
        in_specs=[new_blk] * 3 + [cache_blk] * 2 + [new_blk],
        out_specs=new_blk,
        out_shape=jax.ShapeDtypeStruct((n, SAMPLE_PAD, D_B), BF16),
        compiler_params=_params("arbitrary", "arbitrary"),
        name="attn_sample",
    )(q, k_new, v_new, kt_cache, vt_cache, gb)


def _outproj_kernel(*refs, n_in):
    ins, (w_ref, g_ref, x_ref, out_ref) = refs[:n_in], refs[n_in:]
    y = None
    off = 0
    for r in ins:
        width = r.shape[1]
        part = jnp.dot(r[...], w_ref[off:off + width, :], preferred_element_type=F32)
        y = part if y is None else y + part
        off += width
    yn = y * lax.rsqrt(jnp.mean(y * y, axis=-1, keepdims=True) + NORM_EPS) * g_ref[...]
    out_ref[...] = x_ref[...] + yn


def _outproj(parts, w_bf16, post_g, x2d, tm):
    rows = x2d.shape[0]
    row_spec = pl.BlockSpec((tm, D_MODEL), lambda i: (i, 0))
    return pl.pallas_call(
        functools.partial(_outproj_kernel, n_in=len(parts)),
        grid=(rows // tm,),
        in_specs=[pl.BlockSpec((tm, p.shape[1]), lambda i: (i, 0)) for p in parts]
        + [pl.BlockSpec(w_bf16.shape, lambda i: (0, 0), pipeline_mode=pl.Buffered(1)),
           _const_spec((1, D_MODEL)), row_spec],
        out_specs=row_spec,
        out_shape=jax.ShapeDtypeStruct((rows, D_MODEL), F32),
        compiler_params=_params("arbitrary"),
        name="outproj",
    )(*parts, w_bf16, post_g[None, :], x2d)


def _odd_inproj_kernel(x_ref, g_ref, w_ref, lbp_ref, q_ref, lf_ref, k_ref, v_ref, gt_ref, *, layer):
    x = x_ref[...]
    ms = jnp.mean(x * x, axis=-1, keepdims=True)
    h = (x * lax.rsqrt(ms + NORM_EPS) * g_ref[...]).astype(BF16)

    def seg(j):
        return jnp.dot(h, w_ref[:, j * D_C:(j + 1) * D_C], preferred_element_type=F32)

    lbp = lbp_ref[...]
    e = jnp.exp(lbp - jnp.max(lbp, axis=0, keepdims=True))
    sm = e / jnp.sum(e, axis=0, keepdims=True)
    lb = jnp.sum(sm[1:layer + 1], axis=0, keepdims=True)

    q_ref[...] = _silu(seg(0)).astype(BF16)
    f = lb + (1.0 - lb) * jax.nn.sigmoid(seg(1))
    lf_ref[...] = jnp.log(f)
    k_ref[...] = (1.0 - f).astype(BF16)
    v_ref[...] = seg(2).astype(BF16)
    gt_ref[...] = _silu(seg(3)).astype(BF16)


def _odd_inproj(x2d, pre_g, w_bf16, hgrn_lb, layer, tm):
    rows = x2d.shape[0]
    row_spec = pl.BlockSpec((tm, D_MODEL), lambda i: (i, 0))
    wide = pl.BlockSpec((tm, D_C), lambda i: (i, 0))
    bf = jax.ShapeDtypeStruct((rows, D_C), BF16)
    return pl.pallas_call(
        functools.partial(_odd_inproj_kernel, layer=layer),
        grid=(rows // tm,),
        in_specs=[row_spec, _const_spec((1, D_MODEL)),
                  pl.BlockSpec(w_bf16.shape, lambda i: (0, 0), pipeline_mode=pl.Buffered(1)),
                  _const_spec(hgrn_lb.shape)],
        out_specs=[wide] * 5,
        out_shape=(bf, jax.ShapeDtypeStruct((rows, D_C), F32), bf, bf, bf),
        compiler_params=_params("arbitrary"),
        name="odd_inproj",
    )(x2d, pre_g[None, :], w_bf16, hgrn_lb)


DIAG = 4
SHIFT_PAD = 8

def _hgrn_kernel(q_ref, lf_ref, k_ref, v_ref, gt_ref, gn_ref, s0_ref,
                 o_ref, sfin_ref, st_s, bp_s, kp_s, vp_s, tri_s, split_s, *, tile, t_valid, nt, hb):
    t = pl.program_id(2)

    @pl.when(t == 0)
    def _():
        for h in range(hb):
            st_s[h] = s0_ref[0, h].T

    @pl.when((pl.program_id(0) == 0) & (pl.program_id(1) == 0) & (t == 0))
    def _():
        rr = lax.broadcasted_iota(jnp.int32, (tile, tile), 0)
        cc = lax.broadcasted_iota(jnp.int32, (tile, tile), 1)
        tri_s[...] = jnp.where(rr >= cc, 1.0, 0.0).astype(BF16)
        split_s[...] = 31 - lax.clz(jnp.where(rr > cc, rr ^ cc, 0))

    row = lax.broadcasted_iota(jnp.int32, (tile, LANES), 0)
    tri = tri_s[...]
    split_bit = split_s[...]
    zpad = jnp.zeros((SHIFT_PAD, LANES), F32)
    same_block = [(row % DIAG) >= j for j in range(DIAG)]
    levels = []
    m = DIAG
    while 2 * m <= tile:
        levels.append(m)
        m *= 2

    heads = range(hb)
    cols = [slice(h * LANES, (h + 1) * LANES) for h in heads]
    qf, kf, vb, pieces = [], [], [], []
    for h in heads:
        lf = lf_ref[0, :, cols[h]] * LOG2_E
        k_h = k_ref[0, :, cols[h]].astype(F32)
        if t_valid < tile:
            lf = jnp.where(row < t_valid, lf, 0.0)
            k_h = jnp.where(row < t_valid, k_h, 0.0)
        kf.append(k_h)
        qf.append(q_ref[0, :, cols[h]].astype(F32))
        vb.append(v_ref[0, :, cols[h]])
        p1 = lf.astype(BF16)
        r1 = lf - p1.astype(F32)
        p2 = r1.astype(BF16)
        pieces.append((p1, p2, (r1 - p2.astype(F32)).astype(BF16)))

    b = []
    for h in heads:
        b_h = (jnp.dot(tri, pieces[h][0], preferred_element_type=F32)
               + jnp.dot(tri, pieces[h][1], preferred_element_type=F32)
               + jnp.dot(tri, pieces[h][2], preferred_element_type=F32))
        b.append(b_h)
        bp_s[h, 0:SHIFT_PAD, :] = zpad
        kp_s[h, 0:SHIFT_PAD, :] = zpad
        vp_s[h, 0:SHIFT_PAD, :] = zpad
        bp_s[h, SHIFT_PAD:SHIFT_PAD + tile, :] = b_h
        kp_s[h, SHIFT_PAD:SHIFT_PAD + tile, :] = kf[h]
        vp_s[h, SHIFT_PAD:SHIFT_PAD + tile, :] = vb[h].astype(F32)

    attn = [jnp.zeros((tile, tile), F32) for _ in heads]
    o_block, q_state = [], []
    for h in heads:
        for m in levels:
            nb = tile // (2 * m)
            if nb > 1:
                mid = bp_s[h, pl.ds(SHIFT_PAD + m - 1, nb, stride=2 * m), :]
                mid = jnp.broadcast_to(mid[:, None, :], (nb, 2 * m, LANES)).reshape(tile, LANES)
            else:
                mid = jnp.broadcast_to(bp_s[h, SHIFT_PAD + m - 1:SHIFT_PAD + m, :], (tile, LANES))
            decay = jnp.exp2(-jnp.abs(b[h] - mid))
            attn[h] = jnp.where(split_bit == m.bit_length() - 1,
                                _qk((qf[h] * decay).astype(BF16), (kf[h] * decay).astype(BF16)), attn[h])
        vf = vp_s[h, SHIFT_PAD:SHIFT_PAD + tile, :]
        acc = jnp.sum(qf[h] * kf[h], axis=-1, keepdims=True) * vf
        for j in range(1, DIAG):
            bj = bp_s[h, SHIFT_PAD - j:SHIFT_PAD - j + tile, :]
            kj = kp_s[h, SHIFT_PAD - j:SHIFT_PAD - j + tile, :]
            vj = vp_s[h, SHIFT_PAD - j:SHIFT_PAD - j + tile, :]
            e = jnp.where(same_block[j], b[h] - bj, NEG)
            w = jnp.sum(qf[h] * kj * jnp.exp2(e), axis=-1, keepdims=True)
            acc = acc + w * vj
        o_block.append(acc)
        q_state.append((qf[h] * jnp.exp2(b[h])).astype(BF16))
    o = [o_block[h] + jnp.dot(attn[h].astype(BF16), vb[h], preferred_element_type=F32) for h in heads]

    st = [st_s[h] for h in heads]
    for h in heads:
        o[h] = o[h] + _qk(q_state[h], st[h].astype(BF16))
    for h in heads:
        b_last = bp_s[h, SHIFT_PAD + tile - 1:SHIFT_PAD + tile, :]
        kdec = (kf[h] * jnp.exp2(b_last - b[h])).astype(BF16)
        st_s[h] = jnp.exp2(b_last) * st[h] + lax.dot_general(
            vb[h], kdec, (((0,), (0,)), ((), ())), preferred_element_type=F32)

    for h in heads:
        on = o[h] * lax.rsqrt(jnp.mean(o[h] * o[h], axis=-1, keepdims=True) + NORM_EPS) * gn_ref[...]
        o_ref[0, :, cols[h]] = (on * gt_ref[0, :, cols[h]].astype(F32)).astype(BF16)

    @pl.when(t == nt - 1)
    def _():
        for h in range(hb):
            sfin_ref[0, h] = st_s[h].T


def _hgrn(q, lf, k, v, gt, g_norm, s0, tile, t_valid, hb):
    n, t, _ = q.shape
    nt = t // tile
    blk = pl.BlockSpec((1, tile, hb * LANES), lambda b, h, i: (b, i, h))
    st_blk = pl.BlockSpec((1, hb, HGRN_HEAD, HGRN_HEAD), lambda b, h, i: (b, h, 0, 0))
    pad = pltpu.VMEM((hb, SHIFT_PAD + tile, LANES), F32)
    return pl.pallas_call(
        functools.partial(_hgrn_kernel, tile=tile, t_valid=t_valid, nt=nt, hb=hb),
        grid=(n, N_HEADS_C // hb, nt),
        in_specs=[blk] * 5 + [_const_spec((1, LANES)), st_blk],
        out_specs=[blk, st_blk],
        out_shape=(jax.ShapeDtypeStruct((n, t, D_C), BF16),
                   jax.ShapeDtypeStruct((n, N_HEADS_C, HGRN_HEAD, HGRN_HEAD), F32)),
        scratch_shapes=[pltpu.VMEM((hb, HGRN_HEAD, HGRN_HEAD), F32),
                        pad, pad, pad,
                        pltpu.VMEM((tile, tile), BF16), pltpu.VMEM((tile, tile), jnp.int32)],
        compiler_params=_params("arbitrary", "arbitrary", "arbitrary"),
        name="hgrn",
    )(q, lf, k, v, gt, g_norm[None, :], s0)


def _tiles(n, t):
    rows = n * t
    tm = min(256, rows)
    tm_wide = min(1024, rows)
    tc = min(256, t)
    th = min(128, t)
    return tm, tm_wide, tc, th, N_HEADS_C


def _trunk(x, pos, conv_hist, kv_cache, state0, t_valid, prm):
    n, t, _ = x.shape
    rows = n * t
    tm, tm_wide, tc, th, hb = _tiles(n, t)
    x2d = x.reshape(rows, D_MODEL)
    cos_t, sin_t = _rope_tables(pos)
    if cos_t.shape[0] < tm:
        cos_t = jnp.tile(cos_t, (tm // t, 1))
        sin_t = jnp.tile(sin_t, (tm // t, 1))
    shp = (n, t, D_A)
    conv_prm = (prm["conv_w"], prm["conv_b"], prm["ln_g"], prm["ln_b"])
    if kv_cache is None:
        assert t_valid == t and t % tm == 0
        ya, hist, q, k, v, gb, *kv_t = _even_inproj(x2d, prm["pre0"], prm["w_in_ab"], cos_t, sin_t, tm,
                                                    conv=(t, conv_hist) + conv_prm)
        yb = _attn_prompt(q.reshape(shp), k.reshape(shp), v.reshape(shp), gb.reshape(shp))
    else:
        kv_t = None
        u, ga, q, k, v, gb = _even_inproj(x2d, prm["pre0"], prm["w_in_ab"], cos_t, sin_t, tm)
        ya, hist = _conv_module(u.reshape(shp), ga.reshape(shp), conv_hist, *conv_prm, tc, t_valid)
        yb = _attn_sample(q.reshape(shp), k.reshape(shp), v.reshape(shp), kv_cache[0], kv_cache[1],
                          gb.reshape(shp), t_valid)
    x1 = _outproj([ya.reshape(rows, D_A), yb.reshape(rows, D_B)], prm["w_out_ab"], prm["post0"], x2d, tm_wide)
    qc, lf, kc, vc, gt = _odd_inproj(x1, prm["pre1"], prm["w_in_c"], prm["hgrn_lb"], 1, tm)
    wide = (n, t, D_C)
    oc, s_fin = _hgrn(qc.reshape(wide), lf.reshape(wide), kc.reshape(wide), vc.reshape(wide),
                      gt.reshape(wide), prm["gnorm"], state0, th, t_valid, hb)
    x2 = _outproj([oc.reshape(rows, D_C)], prm["w_out_c"], prm["post1"], x1, tm_wide)
    if kv_t is None:
        kv_t = (k.reshape(shp), v.reshape(shp))
    return x2.reshape(n, t, D_MODEL), hist, kv_t[0], kv_t[1], s_fin


def kernel(x_prompt, x_sample, cache_conv, cache_swa_k, cache_swa_v, state_hgrn, pre_norm, post_norm,
           w_in_ab, w_out_ab, conv_w, conv_b, conv_ln_g, conv_ln_b, w_in_c, w_out_c, hgrn_gnorm, hgrn_lb):
    n_p, seq_p, _ = x_prompt.shape
    n_s, seq_s, _ = x_sample.shape
    hist_rows = CONV_WIDTH - 1
    prm = dict(pre0=pre_norm[0], pre1=pre_norm[1], post0=post_norm[0], post1=post_norm[1],
               w_in_ab=w_in_ab[0].astype(BF16), w_out_ab=w_out_ab[0].astype(BF16),
               conv_w=conv_w[0], conv_b=conv_b[0], ln_g=conv_ln_g[0], ln_b=conv_ln_b[0],
               w_in_c=w_in_c[0].astype(BF16), w_out_c=w_out_c[0].astype(BF16),
               gnorm=hgrn_gnorm[0], hgrn_lb=hgrn_lb)

    pos_p = jnp.arange(seq_p, dtype=jnp.int32)
    yp, hist_p, k_p, v_p, s_p = _trunk(
        x_prompt, pos_p, jnp.zeros((n_p, HALO, D_A), F32), None,
        jnp.zeros((n_p, N_HEADS_C, HGRN_HEAD, HGRN_HEAD), F32), seq_p, prm)

    pad = SAMPLE_PAD - seq_s
    xs = jnp.pad(x_sample, ((0, 0), (0, pad), (0, 0)))
    pos_s = PAST_LEN + jnp.arange(SAMPLE_PAD, dtype=jnp.int32)
    hist_s = jnp.pad(cache_conv[0], ((0, 0), (HALO - hist_rows, 0), (0, 0)))
    kv = (jnp.transpose(cache_swa_k[0], (0, 2, 3, 1)), jnp.transpose(cache_swa_v[0], (0, 2, 3, 1)))
    ys, hist_s, k_s, v_s, s_s = _trunk(
        xs, pos_s, hist_s, kv, state_hgrn[0], seq_s, prm)

    heads = (N_HEADS_B, HEAD_DIM_B)
    return (yp, ys[:, :seq_s],
            hist_p[None, :, HALO - hist_rows:], hist_s[None, :, HALO - hist_rows:],
            jnp.transpose(k_p, (0, 3, 1, 2))[None], jnp.transpose(v_p, (0, 3, 1, 2))[None],
            k_s[:, :seq_s].reshape(1, n_s, seq_s, *heads), v_s[:, :seq_s].reshape(1, n_s, seq_s, *heads),
            s_p[None], s_s[None])
```

```python
import functools

import jax
import jax.numpy as jnp
from jax import lax
from jax.experimental import pallas as pl
from jax.experimental.pallas import tpu as pltpu

F32 = jnp.float32
BF16 = jnp.bfloat16

D_MODEL = 1024
D_A = 1024
D_B = 1024
CONV_WIDTH = 31
HALO = 32
HEAD_DIM_B = 64
N_HEADS_B = 16
LANES = 128
SUBLANES = 8
N_PAIRS = D_B // LANES
DSWA_CONFIGS = ((128, 1), (512, 4), (2048, 16))
SPAN = 128
ROPE_THETA = 10000.0
PAST_LEN = 16384
HGRN_HEAD = 128
N_HEADS_C = 16
D_C = N_HEADS_C * HGRN_HEAD
NORM_EPS = 1e-6
NEG = -1e30
LOG2_E = 1.4426950408889634
SAMPLE_PAD = 16
ATTN_GROUP = 4
VMEM_LIMIT = 56 * 1024 * 1024


def _silu(x):
    return x * jax.nn.sigmoid(x)


def _params(*sem):
    return pltpu.CompilerParams(dimension_semantics=sem, vmem_limit_bytes=VMEM_LIMIT)


def _const_spec(shape):
    return pl.BlockSpec(shape, lambda *_: (0,) * len(shape))


def _rope_table_kernel(pos_ref, invf_ref, sign_ref, cos_ref, sin_ref):
    ang = pos_ref[...] * invf_ref[...]
    cos_ref[...] = jnp.cos(ang)
    sin_ref[...] = jnp.sin(ang) * sign_ref[...]


def _rope_tables(pos):
    t = pos.shape[0]
    half = HEAD_DIM_B // 2
    inv_freq = ROPE_THETA ** (-jnp.arange(half, dtype=F32) / half)
    invf = jnp.tile(inv_freq, LANES // half)[None, :]
    lane = jnp.arange(LANES)
    sign = jnp.where(lane % HEAD_DIM_B < half, -1.0, 1.0).astype(F32)[None, :]
    posb = jnp.broadcast_to(pos.astype(F32)[:, None], (t, LANES))
    return pl.pallas_call(
        _rope_table_kernel,
        out_shape=(jax.ShapeDtypeStruct((t, LANES), F32),) * 2,
        name="rope_tables",
    )(posb, invf, sign)


def _even_inproj_kernel(*refs, conv_tiles):
    x_ref, g_ref, w_ref, cos_ref, sin_ref = refs[:5]
    if conv_tiles is None:
        u_ref, ga_ref, q_ref, k_ref, v_ref, gb_ref = refs[5:]
        kt_ref = vt_ref = None
    else:
        buf_ref, cw_ref, cb_ref, lg_ref, lb_ref = refs[5:10]
        ya_ref, tail_ref, q_ref, k_ref, v_ref, gb_ref, kt_ref, vt_ref = refs[10:18]
        (ext_ref,) = refs[18:]
        if conv_tiles > 1:
            @pl.when(pl.program_id(0) == 0)
            def _():
                ext_ref[...] = jnp.zeros(ext_ref.shape, F32)
    x = x_ref[...]
    ms = jnp.mean(x * x, axis=-1, keepdims=True)
    h = (x * lax.rsqrt(ms + NORM_EPS) * g_ref[...]).astype(BF16)

    def seg(j):
        return jnp.dot(h, w_ref[:, j * D_A:(j + 1) * D_A], preferred_element_type=F32)

    cos = cos_ref[...]
    sin = sin_ref[...]
    lane = lax.broadcasted_iota(jnp.int32, cos.shape, 1)
    first_half = (lane % HEAD_DIM_B) < (HEAD_DIM_B // 2)

    def emit(val, out_ref, rotate, t_ref, pairs=range(N_PAIRS)):
        for c in pairs:
            xc = val[:, c * LANES:(c + 1) * LANES]
            if rotate:
                partner = jnp.where(first_half,
                                    pltpu.roll(xc, LANES - HEAD_DIM_B // 2, 1),
                                    pltpu.roll(xc, HEAD_DIM_B // 2, 1))
                xc = xc * cos + partner * sin
            out_ref[:, c * LANES:(c + 1) * LANES] = xc
            if t_ref is not None:
                xt = xc.T
                t_ref[0, 2 * c] = xt[0:HEAD_DIM_B]
                t_ref[0, 2 * c + 1] = xt[HEAD_DIM_B:LANES]
        return xc[0:SUBLANES]

    def emit_gate(val, cols):
        g = _silu(val[:, cols])
        gb_ref[:, cols] = g.astype(BF16)
        return g[0:SUBLANES, 0:LANES]

    def glu_and_gate():
        return seg(0) * jax.nn.sigmoid(seg(1)), _silu(seg(2))

    if conv_tiles is None:
        u, gate = glu_and_gate()
        u_ref[...] = u
        ga_ref[...] = gate.astype(BF16)
        emit(seg(3), q_ref, True, None)
        emit(seg(4), k_ref, True, kt_ref)
        emit(seg(5), v_ref, False, vt_ref)
        emit_gate(seg(6), slice(0, D_B))
    else:
        u, gate = glu_and_gate()
        vals = {}

        def piece(j, half):
            def run():
                if j not in vals:
                    vals[j] = seg(j)
                pairs = range(half * N_PAIRS // 2, (half + 1) * N_PAIRS // 2)
                if j == 3:
                    return emit(vals[j], q_ref, True, None, pairs)
                if j == 4:
                    return emit(vals[j], k_ref, True, kt_ref, pairs)
                if j == 5:
                    return emit(vals[j], v_ref, False, vt_ref, pairs)
                return emit_gate(vals[j], slice(half * D_B // 2, (half + 1) * D_B // 2))
            return run

        tc = x.shape[0]
        ya = _conv_tile(pl.program_id(0) % conv_tiles, u, gate, buf_ref, cw_ref, cb_ref, lg_ref, lb_ref,
                        tail_ref, ext_ref, tc=tc, t_valid=tc, nt=conv_tiles,
                        between=[piece(j, half) for j in (3, 4, 5, 6) for half in (0, 1)])
        ya_ref[...] = ya.astype(BF16)


def _even_inproj(x2d, pre_g, w_bf16, cos_t, sin_t, tm, conv=None):
    rows = x2d.shape[0]
    nt_tab = cos_t.shape[0] // tm
    row_spec = pl.BlockSpec((tm, D_MODEL), lambda i: (i, 0))
    tab_spec = pl.BlockSpec((tm, LANES), lambda i: (i % nt_tab, 0))
    f32_out = jax.ShapeDtypeStruct((rows, D_A), F32)
    bf_out = jax.ShapeDtypeStruct((rows, D_A), BF16)
    in_specs = [row_spec, _const_spec((1, D_MODEL)),
                pl.BlockSpec(w_bf16.shape, lambda i: (0, 0), pipeline_mode=pl.Buffered(1)),
                tab_spec, tab_spec]
    operands = [x2d, pre_g[None, :], w_bf16, cos_t, sin_t]
    if conv is None:
        conv_tiles = None
        out_specs = [row_spec] * 6
        out_shape = (f32_out, bf_out, f32_out, f32_out, f32_out, bf_out)
        scratch = []
    else:
        seq, hist, conv_w, conv_b, ln_g, ln_b = conv
        n = rows // seq
        conv_tiles = seq // tm
        halo = pl.BlockSpec((1, HALO, D_A), lambda i: (i // conv_tiles, 0, 0))
        t_spec = pl.BlockSpec((1, N_HEADS_B, HEAD_DIM_B, tm), lambda i: (i // conv_tiles, 0, 0, i % conv_tiles))
        t_out = jax.ShapeDtypeStruct((n, N_HEADS_B, HEAD_DIM_B, seq), F32)
        in_specs += [halo, _const_spec((CONV_WIDTH, D_A))] + [_const_spec((1, D_A))] * 3
        operands += [hist, conv_w, conv_b[None, :], ln_g[None, :], ln_b[None, :]]
        out_specs = [row_spec, halo] + [row_spec] * 4 + [t_spec] * 2
        out_shape = (bf_out, jax.ShapeDtypeStruct((n, HALO, D_A), F32),
                     f32_out, f32_out, f32_out, bf_out, t_out, t_out)
        scratch = _conv_scratch(tm)
    return pl.pallas_call(
        functools.partial(_even_inproj_kernel, conv_tiles=conv_tiles),
        grid=(rows // tm,),
        in_specs=in_specs,
        out_specs=out_specs,
        out_shape=out_shape,
        scratch_shapes=scratch,
        compiler_params=_params("arbitrary"),
        name="even_inproj",
    )(*operands)


def _conv_tile(t, u, gate, buf_ref, w_ref, cb_ref, lg_ref, lb_ref, tail_ref, ext_ref,
               *, tc, t_valid, nt, between=()):
    between = list(between)
    n_chunks = D_A // LANES
    if nt > 1:
        ext_ref[0:HALO, :] = jnp.where(t == 0, buf_ref[0], ext_ref[tc:tc + HALO, :])
    else:
        ext_ref[0:HALO, :] = buf_ref[0]
    ext_ref[HALO:HALO + tc, :] = u
    ext_ref[HALO + tc:HALO + tc + SUBLANES, :] = jnp.zeros((SUBLANES, D_A), F32)
    first = HALO - (CONV_WIDTH - 1)
    chunks = []
    anchor = None
    for c in range(n_chunks):
        for j, work in enumerate(between):
            if j * n_chunks // len(between) == c:
                anchor = work()
        cols = slice(c * LANES, (c + 1) * LANES)
        acc = jnp.zeros((tc, LANES), F32)
        if anchor is not None:
            bits = lax.shift_right_logical(lax.bitcast_convert_type(anchor, jnp.uint32), jnp.uint32(32))
            acc = acc + jnp.tile(lax.bitcast_convert_type(bits, F32), (tc // SUBLANES, 1))
        for rho in range(SUBLANES):
            part = None
            for k in range(CONV_WIDTH):
                if (first + k) % SUBLANES == rho:
                    base = first + k - rho
                    term = ext_ref[base:base + tc + SUBLANES, cols] * w_ref[k:k + 1, cols]
                    part = term if part is None else part + term
            if part is None:
                continue
            acc = acc + part[rho:rho + tc]
        chunks.append(acc + cb_ref[:, cols])
    y = jnp.concatenate(chunks, axis=1)
    yc = y - jnp.mean(y, axis=-1, keepdims=True)
    yn = yc * lax.rsqrt(jnp.mean(yc * yc, axis=-1, keepdims=True) + NORM_EPS)
    yn = yn * lg_ref[...] + lb_ref[...]
    tail_ref[0] = ext_ref[t_valid:t_valid + HALO, :]
    return _silu(yn) * gate


def _conv_scratch(tc):
    return [pltpu.VMEM((HALO + tc + SUBLANES, D_A), F32)]


def _conv_kernel(u_ref, ga_ref, buf_ref, w_ref, cb_ref, lg_ref, lb_ref,
                 ya_ref, tail_ref, ext_ref, *, tc, t_valid, nt):
    if nt > 1:
        @pl.when((pl.program_id(0) == 0) & (pl.program_id(1) == 0))
        def _():
            ext_ref[...] = jnp.zeros(ext_ref.shape, F32)
    ya = _conv_tile(pl.program_id(1), u_ref[0], ga_ref[0].astype(F32), buf_ref, w_ref, cb_ref, lg_ref, lb_ref,
                    tail_ref, ext_ref, tc=tc, t_valid=t_valid, nt=nt)
    ya_ref[0] = ya.astype(BF16)


def _conv_module(u, ga, buf, conv_w, conv_b, ln_g, ln_b, tc, t_valid):
    n, t, _ = u.shape
    nt = t // tc
    blk = pl.BlockSpec((1, tc, D_A), lambda b, i: (b, i, 0))
    halo = pl.BlockSpec((1, HALO, D_A), lambda b, i: (b, 0, 0))
    last_valid = t_valid - (nt - 1) * tc
    return pl.pallas_call(
        functools.partial(_conv_kernel, tc=tc, t_valid=last_valid, nt=nt),
        grid=(n, nt),
        in_specs=[blk, blk, halo, _const_spec((CONV_WIDTH, D_A)),
                  _const_spec((1, D_A)), _const_spec((1, D_A)), _const_spec((1, D_A))],
        out_specs=[blk, halo],
        out_shape=(jax.ShapeDtypeStruct((n, t, D_A), BF16),
                   jax.ShapeDtypeStruct((n, HALO, D_A), F32)),
        scratch_shapes=_conv_scratch(tc),
        compiler_params=_params("arbitrary", "arbitrary"),
        name="conv_module",
    )(u, ga, buf, conv_w, conv_b[None, :], ln_g[None, :], ln_b[None, :])


def _qk(q, k):
    return lax.dot_general(q, k, (((1,), (1,)), ((), ())), preferred_element_type=F32)


def _attn_prompt_kernel(q_ref, k_ref, v_ref, gb_ref, out_ref, q_s, k_s, v_s, o_s, m_s, l_s, *, seq):
    nblk = seq // SPAN
    scale = HEAD_DIM_B ** -0.5 * LOG2_E
    lo = lax.broadcasted_iota(jnp.int32, (SPAN, LANES), 1) < HEAD_DIM_B

    def rows(c, blk):
        dil = DSWA_CONFIGS[c][1]
        per_seq = nblk // dil
        return dil * SPAN * (blk % per_seq) + blk // per_seq, dil

    def take(ref, start, stride):
        if stride == 1:
            return ref[0, start:start + SPAN, :]
        return ref[0, pl.ds(start, SPAN, stride=stride), :]

    for c in range(len(DSWA_CONFIGS)):
        for blk in range(nblk):
            start, stride = rows(c, blk)
            qb = take(q_ref, start, stride) * scale
            q_s[c, blk, 0:SPAN, :] = jnp.where(lo, qb, 0.0).astype(BF16)
            q_s[c, blk, SPAN:2 * SPAN, :] = jnp.where(lo, 0.0, qb).astype(BF16)
            k_s[c, blk] = take(k_ref, start, stride).astype(BF16)
            vb = take(v_ref, start, stride)
            v_s[c, blk, 0] = jnp.where(lo, vb, 1.0).astype(BF16)
            v_s[c, blk, 1] = jnp.where(lo, 1.0, vb).astype(BF16)

    row = lax.broadcasted_iota(jnp.int32, (2 * SPAN, SPAN), 0) % SPAN
    col = lax.broadcasted_iota(jnp.int32, (2 * SPAN, SPAN), 1)
    upper = col >= row
    diag = col == row
    lower = col < row

    def has_prev(c, blk):
        return blk % (nblk // DSWA_CONFIGS[c][1]) > 0

    def scores(c, blk):
        q2 = q_s[c, blk]
        s_own = _qk(q2, k_s[c, blk])
        s_far = _qk(q2, k_s[c, blk - 1]) if has_prev(c, blk) else NEG
        return s_own, s_far

    def probabilities(s_own, s_far):
        s_self = jnp.sum(jnp.where(diag, s_own, 0.0), axis=-1, keepdims=True)
        s = jnp.where(upper, s_far, s_own)
        m = jnp.maximum(jnp.max(s, axis=-1, keepdims=True), s_self)
        p = jnp.exp2(s - m)
        p_self = jnp.exp2(s_self - m)
        p_own = jnp.where(lower, p, jnp.where(diag, p_self, 0.0)).astype(BF16)
        p_far = jnp.where(upper, p, 0.0).astype(BF16)
        return p_own, p_far, m

    def finish(c, blk, p_own, p_far, m):
        o = []
        for h in range(2):
            ph = slice(h * SPAN, (h + 1) * SPAN)
            o_h = jnp.dot(p_own[ph], v_s[c, blk, h], preferred_element_type=F32)
            if has_prev(c, blk):
                o_h = o_h + jnp.dot(p_far[ph], v_s[c, blk - 1, h], preferred_element_type=F32)
            o.append(o_h)
        m = jnp.broadcast_to(m, (2 * SPAN, LANES))
        start, stride = rows(c, blk)
        dst = pl.ds(start, SPAN, stride=stride) if stride > 1 else pl.ds(start, SPAN)
        o_s[c, dst, :] = jnp.where(lo, o[0], o[1])
        m_s[c, dst, :] = jnp.where(lo, m[0:SPAN], m[SPAN:2 * SPAN])
        l_s[c, dst, :] = jnp.where(lo, o[1], o[0])

    todo = [(c, blk) for c in range(len(DSWA_CONFIGS)) for blk in range(nblk)]
    groups = [todo[i:i + ATTN_GROUP] for i in range(0, len(todo), ATTN_GROUP)]
    pending = [scores(*cb) for cb in groups[0]]
    for gi, group in enumerate(groups):
        ahead = [scores(*cb) for cb in groups[gi + 1]] if gi + 1 < len(groups) else []
        probs = [probabilities(*sc) for sc in pending]
        for cb, pr in zip(group, probs):
            finish(*cb, *pr)
        pending = ahead

    ma, mb, mc = m_s[0], m_s[1], m_s[2]
    mx = jnp.maximum(jnp.maximum(ma, mb), mc)
    ea, eb, ec = jnp.exp2(ma - mx), jnp.exp2(mb - mx), jnp.exp2(mc - mx)
    la, lb, lc = (pltpu.roll(l_s[c], HEAD_DIM_B, 1) for c in range(3))
    merged = (ea * o_s[0] + eb * o_s[1] + ec * o_s[2]) / (ea * la + eb * lb + ec * lc)
    out_ref[0] = (merged * gb_ref[0].astype(F32)).astype(BF16)


def _attn_prompt(q, k, v, gb):
    n, seq, _ = q.shape
    nblk = seq // SPAN
    assert all(w // d == SPAN and nblk % d == 0 for w, d in DSWA_CONFIGS)
    ncfg = len(DSWA_CONFIGS)
    blk = pl.BlockSpec((1, seq, LANES), lambda b, h: (b, 0, h))
    return pl.pallas_call(
        functools.partial(_attn_prompt_kernel, seq=seq),
        grid=(n, N_PAIRS),
        in_specs=[blk] * 4,
        out_specs=blk,
        out_shape=jax.ShapeDtypeStruct((n, seq, D_B), BF16),
        scratch_shapes=[pltpu.VMEM((ncfg, nblk, 2 * SPAN, LANES), BF16),
                        pltpu.VMEM((ncfg, nblk, SPAN, LANES), BF16),
                        pltpu.VMEM((ncfg, nblk, 2, SPAN, LANES), BF16),
                        pltpu.VMEM((ncfg, seq, LANES), F32),
                        pltpu.VMEM((ncfg, seq, LANES), F32),
                        pltpu.VMEM((ncfg, seq, LANES), F32)],
        compiler_params=_params("arbitrary", "arbitrary"),
        name="attn_prompt",
    )(q, k, v, gb)


def _attn_sample_kernel(q_ref, kn_ref, vn_ref, kt_ref, vt_ref, gb_ref, out_ref, *, t_new, n_past, heads):
    nq = SAMPLE_PAD
    width = heads * HEAD_DIM_B
    scale = HEAD_DIM_B ** -0.5
    q_all = q_ref[0] * scale
    k_new = kn_ref[0]
    v_new = vn_ref[0]

    lane = lax.broadcasted_iota(jnp.int32, (nq, width), 1)
    qm = jnp.concatenate([jnp.where(lane // HEAD_DIM_B == h, q_all, 0.0) for h in range(heads)], axis=0)
    s_new = _qk(qm.astype(BF16), k_new.astype(BF16))
    t_row = lax.broadcasted_iota(jnp.int32, (nq, nq), 0)
    t_col = lax.broadcasted_iota(jnp.int32, (nq, nq), 1)

    dist = (n_past + lax.broadcasted_iota(jnp.int32, (nq, n_past), 0)
            - lax.broadcasted_iota(jnp.int32, (nq, n_past), 1))
    far = 1 << 30
    masks, starts = [], []
    for window, dil in DSWA_CONFIGS:
        start = n_past - window
        d = dist[:, start:]
        masks.append(jnp.where((d & (dil - 1)) == 0, d, far) <= window)
        starts.append(start)

    new_ok = [jnp.where(((t_row - t_col) & (dil - 1)) == 0, t_row - t_col, -1) >= 0 for _, dil in DSWA_CONFIGS]
    ncfg = len(DSWA_CONFIGS)
    cols = [slice(h * HEAD_DIM_B, (h + 1) * HEAD_DIM_B) for h in range(heads)]

    s_buf = [jnp.dot(q_all[:, cols[h]].astype(BF16), kt_ref[0, h].astype(BF16), preferred_element_type=F32)
             for h in range(heads)]
    stats = []
    for h in range(heads):
        sn = s_new[h * nq:(h + 1) * nq, :]
        per_cfg = []
        for c in range(ncfg):
            sc = jnp.where(masks[c], s_buf[h][:, starts[c]:], NEG)
            snm = jnp.where(new_ok[c], sn, NEG)
            m = jnp.maximum(jnp.max(sc, axis=-1, keepdims=True), jnp.max(snm, axis=-1, keepdims=True))
            p = jnp.exp(sc - m)
            pn = jnp.exp(snm - m)
            l = jnp.sum(p, axis=-1, keepdims=True) + jnp.sum(pn, axis=-1, keepdims=True)
            per_cfg.append((p.astype(BF16), pn, l, m + jnp.log(l)))
        stats.append(per_cfg)
    outs = []
    for h in range(heads):
        vt = vt_ref[0, h].astype(BF16)
        vnh = v_new[:, cols[h]]
        acc = []
        for c in range(ncfg):
            p, pn, l, _ = stats[h][c]
            o = _qk(p, vt[:, starts[c]:])
            for tn in range(t_new):
                o = o + pn[:, tn:tn + 1] * vnh[tn:tn + 1, :]
            acc.append(o / l)
        den = [stats[h][c][3] for c in range(ncfg)]
        mx = jnp.maximum(jnp.maximum(den[0], den[1]), den[2])
        e = [jnp.exp(d - mx) for d in den]
        outs.append((e[0] * acc[0] + e[1] * acc[1] + e[2] * acc[2]) / (e[0] + e[1] + e[2]))
    merged = jnp.concatenate(outs, axis=1)
    out_ref[0] = (merged * gb_ref[0].astype(F32)).astype(BF16)


def _attn_sample(q, k_new, v_new, kt_cache, vt_cache, gb, t_new, heads=N_HEADS_B):
    n, _, _, n_past = kt_cache.shape
    assert n_past >= DSWA_CONFIGS[2][0] and t_new <= min(d for _, d in DSWA_CONFIGS[1:]) and t_new <= SAMPLE_PAD
    width = heads * HEAD_DIM_B
    new_blk = pl.BlockSpec((1, SAMPLE_PAD, width), lambda b, h: (b, 0, h))
    cache_blk = pl.BlockSpec((1, heads, HEAD_DIM_B, n_past), lambda b, h: (b, h, 0, 0))
    return pl.pallas_call(
        functools.partial(_attn_sample_kernel, t_new=t_new, n_past=n_past, heads=heads),
        grid=(n, N_HEADS_B // heads),
        in_specs=[new_blk] * 3 + [cache_blk] * 2 + [new_blk],
        out_specs=new_blk,
        out_shape=jax.ShapeDtypeStruct((n, SAMPLE_PAD, D_B), BF16),
        compiler_params=_params("arbitrary", "arbitrary"),
        name="attn_sample",
    )(q, k_new, v_new, kt_cache, vt_cache, gb)


def _outproj_kernel(*refs, n_in):
    ins, (w_ref, g_ref, x_ref, out_ref) = refs[:n_in], refs[n_in:]
    y = None
    off = 0
    for r in ins:
        width = r.shape[1]
        part = jnp.dot(r[...], w_ref[off:off + width, :], preferred_element_type=F32)
        y = part if y is None else y + part
        off += width
    yn = y * lax.rsqrt(jnp.mean(y * y, axis=-1, keepdims=True) + NORM_EPS) * g_ref[...]
    out_ref[...] = x_ref[...] + yn


def _outproj(parts, w_bf16, post_g, x2d, tm):
    rows = x2d.shape[0]
    row_spec = pl.BlockSpec((tm, D_MODEL), lambda i: (i, 0))
    return pl.pallas_call(
        functools.partial(_outproj_kernel, n_in=len(parts)),
        grid=(rows // tm,),
        in_specs=[pl.BlockSpec((tm, p.shape[1]), lambda i: (i, 0)) for p in parts]
        + [pl.BlockSpec(w_bf16.shape, lambda i: (0, 0), pipeline_mode=pl.Buffered(1)),
           _const_spec((1, D_MODEL)), row_spec],
        out_specs=row_spec,
        out_shape=jax.ShapeDtypeStruct((rows, D_MODEL), F32),
        compiler_params=_params("arbitrary"),
        name="outproj",
    )(*parts, w_bf16, post_g[None, :], x2d)


def _odd_inproj_kernel(x_ref, g_ref, w_ref, lbp_ref, q_ref, lf_ref, k_ref, v_ref, gt_ref, *, layer):
    x = x_ref[...]
    ms = jnp.mean(x * x, axis=-1, keepdims=True)
    h = (x * lax.rsqrt(ms + NORM_EPS) * g_ref[...]).astype(BF16)

    def seg(j):
        return jnp.dot(h, w_ref[:, j * D_C:(j + 1) * D_C], preferred_element_type=F32)

    lbp = lbp_ref[...]
    e = jnp.exp(lbp - jnp.max(lbp, axis=0, keepdims=True))
    sm = e / jnp.sum(e, axis=0, keepdims=True)
    lb = jnp.sum(sm[1:layer + 1], axis=0, keepdims=True)

    q_ref[...] = _silu(seg(0)).astype(BF16)
    f = lb + (1.0 - lb) * jax.nn.sigmoid(seg(1))
    lf_ref[...] = jnp.log(f)
    k_ref[...] = (1.0 - f).astype(BF16)
    v_ref[...] = seg(2).astype(BF16)
    gt_ref[...] = _silu(seg(3)).astype(BF16)


def _odd_inproj(x2d, pre_g, w_bf16, hgrn_lb, layer, tm):
    rows = x2d.shape[0]
    row_spec = pl.BlockSpec((tm, D_MODEL), lambda i: (i, 0))
    wide = pl.BlockSpec((tm, D_C), lambda i: (i, 0))
    bf = jax.ShapeDtypeStruct((rows, D_C), BF16)
    return pl.pallas_call(
        functools.partial(_odd_inproj_kernel, layer=layer),
        grid=(rows // tm,),
        in_specs=[row_spec, _const_spec((1, D_MODEL)),
                  pl.BlockSpec(w_bf16.shape, lambda i: (0, 0), pipeline_mode=pl.Buffered(1)),
                  _const_spec(hgrn_lb.shape)],
        out_specs=[wide] * 5,
        out_shape=(bf, jax.ShapeDtypeStruct((rows, D_C), F32), bf, bf, bf),
        compiler_params=_params("arbitrary"),
        name="odd_inproj",
    )(x2d, pre_g[None, :], w_bf16, hgrn_lb)


DIAG = 4
SHIFT_PAD = 8

def _hgrn_kernel(q_ref, lf_ref, k_ref, v_ref, gt_ref, gn_ref, s0_ref,
                 o_ref, sfin_ref, st_s, bp_s, kp_s, vp_s, tri_s, split_s, *, tile, t_valid, nt, hb):
    t = pl.program_id(2)

    @pl.when(t == 0)
    def _():
        for h in range(hb):
            st_s[h] = s0_ref[0, h].T

    @pl.when((pl.program_id(0) == 0) & (pl.program_id(1) == 0) & (t == 0))
    def _():
        rr = lax.broadcasted_iota(jnp.int32, (tile, tile), 0)
        cc = lax.broadcasted_iota(jnp.int32, (tile, tile), 1)
        tri_s[...] = jnp.where(rr >= cc, 1.0, 0.0).astype(BF16)
        split_s[...] = 31 - lax.clz(jnp.where(rr > cc, rr ^ cc, 0))

    row = lax.broadcasted_iota(jnp.int32, (tile, LANES), 0)
    tri = tri_s[...]
    split_bit = split_s[...]
    zpad = jnp.zeros((SHIFT_PAD, LANES), F32)
    same_block = [(row % DIAG) >= j for j in range(DIAG)]
    levels = []
    m = DIAG
    while 2 * m <= tile:
        levels.append(m)
        m *= 2

    heads = range(hb)
    cols = [slice(h * LANES, (h + 1) * LANES) for h in heads]
    qf, kf, vb, pieces = [], [], [], []
    for h in heads:
        lf = lf_ref[0, :, cols[h]] * LOG2_E
        k_h = k_ref[0, :, cols[h]].astype(F32)
        if t_valid < tile:
            lf = jnp.where(row < t_valid, lf, 0.0)
            k_h = jnp.where(row < t_valid, k_h, 0.0)
        kf.append(k_h)
        qf.append(q_ref[0, :, cols[h]].astype(F32))
        vb.append(v_ref[0, :, cols[h]])
        p1 = lf.astype(BF16)
        r1 = lf - p1.astype(F32)
        p2 = r1.astype(BF16)
        pieces.append((p1, p2, (r1 - p2.astype(F32)).astype(BF16)))

    b = []
    for h in heads:
        b_h = (jnp.dot(tri, pieces[h][0], preferred_element_type=F32)
               + jnp.dot(tri, pieces[h][1], preferred_element_type=F32)
               + jnp.dot(tri, pieces[h][2], preferred_element_type=F32))
        b.append(b_h)
        bp_s[h, 0:SHIFT_PAD, :] = zpad
        kp_s[h, 0:SHIFT_PAD, :] = zpad
        vp_s[h, 0:SHIFT_PAD, :] = zpad
        bp_s[h, SHIFT_PAD:SHIFT_PAD + tile, :] = b_h
        kp_s[h, SHIFT_PAD:SHIFT_PAD + tile, :] = kf[h]
        vp_s[h, SHIFT_PAD:SHIFT_PAD + tile, :] = vb[h].astype(F32)

    attn = [jnp.zeros((tile, tile), F32) for _ in heads]
    o_block, q_state = [], []
    for h in heads:
        for m in levels:
            nb = tile // (2 * m)
            if nb > 1:
                mid = bp_s[h, pl.ds(SHIFT_PAD + m - 1, nb, stride=2 * m), :]
                mid = jnp.broadcast_to(mid[:, None, :], (nb, 2 * m, LANES)).reshape(tile, LANES)
            else:
                mid = jnp.broadcast_to(bp_s[h, SHIFT_PAD + m - 1:SHIFT_PAD + m, :], (tile, LANES))
            decay = jnp.exp2(-jnp.abs(b[h] - mid))
            attn[h] = jnp.where(split_bit == m.bit_length() - 1,
                                _qk((qf[h] * decay).astype(BF16), (kf[h] * decay).astype(BF16)), attn[h])
        vf = vp_s[h, SHIFT_PAD:SHIFT_PAD + tile, :]
        acc = jnp.sum(qf[h] * kf[h], axis=-1, keepdims=True) * vf
        for j in range(1, DIAG):
            bj = bp_s[h, SHIFT_PAD - j:SHIFT_PAD - j + tile, :]
            kj = kp_s[h, SHIFT_PAD - j:SHIFT_PAD - j + tile, :]
            vj = vp_s[h, SHIFT_PAD - j:SHIFT_PAD - j + tile, :]
            e = jnp.where(same_block[j], b[h] - bj, NEG)
            w = jnp.sum(qf[h] * kj * jnp.exp2(e), axis=-1, keepdims=True)
            acc = acc + w * vj
        o_block.append(acc)
        q_state.append((qf[h] * jnp.exp2(b[h])).astype(BF16))
    o = [o_block[h] + jnp.dot(attn[h].astype(BF16), vb[h], preferred_element_type=F32) for h in heads]

    st = [st_s[h] for h in heads]
    for h in heads:
        o[h] = o[h] + _qk(q_state[h], st[h].astype(BF16))
    for h in heads:
        b_last = bp_s[h, SHIFT_PAD + tile - 1:SHIFT_PAD + tile, :]
        kdec = (kf[h] * jnp.exp2(b_last - b[h])).astype(BF16)
        st_s[h] = jnp.exp2(b_last) * st[h] + lax.dot_general(
            vb[h], kdec, (((0,), (0,)), ((), ())), preferred_element_type=F32)

    for h in heads:
        on = o[h] * lax.rsqrt(jnp.mean(o[h] * o[h], axis=-1, keepdims=True) + NORM_EPS) * gn_ref[...]
        o_ref[0, :, cols[h]] = (on * gt_ref[0, :, cols[h]].astype(F32)).astype(BF16)

    @pl.when(t == nt - 1)
    def _():
        for h in range(hb):
            sfin_ref[0, h] = st_s[h].T


def _hgrn(q, lf, k, v, gt, g_norm, s0, tile, t_valid, hb):
    n, t, _ = q.shape
    nt = t // tile
    blk = pl.BlockSpec((1, tile, hb * LANES), lambda b, h, i: (b, i, h))
    st_blk = pl.BlockSpec((1, hb, HGRN_HEAD, HGRN_HEAD), lambda b, h, i: (b, h, 0, 0))
    pad = pltpu.VMEM((hb, SHIFT_PAD + tile, LANES), F32)
    return pl.pallas_call(
        functools.partial(_hgrn_kernel, tile=tile, t_valid=t_valid, nt=nt, hb=hb),
        grid=(n, N_HEADS_C // hb, nt),
        in_specs=[blk] * 5 + [_const_spec((1, LANES)), st_blk],
        out_specs=[blk, st_blk],
        out_shape=(jax.ShapeDtypeStruct((n, t, D_C), BF16),
                   jax.ShapeDtypeStruct((n, N_HEADS_C, HGRN_HEAD, HGRN_HEAD), F32)),
        scratch_shapes=[pltpu.VMEM((hb, HGRN_HEAD, HGRN_HEAD), F32),
                        pad, pad, pad,
                        pltpu.VMEM((tile, tile), BF16), pltpu.VMEM((tile, tile), jnp.int32)],
        compiler_params=_params("arbitrary", "arbitrary", "arbitrary"),
        name="hgrn",
    )(q, lf, k, v, gt, g_norm[None, :], s0)


def _tiles(n, t):
    rows = n * t
    tm = min(256, rows)
    tm_wide = min(1024, rows)
    tc = min(256, t)
    th = min(128, t)
    return tm, tm_wide, tc, th, N_HEADS_C


def _trunk(x, pos, conv_hist, kv_cache, state0, t_valid, prm):
    n, t, _ = x.shape
    rows = n * t
    tm, tm_wide, tc, th, hb = _tiles(n, t)
    x2d = x.reshape(rows, D_MODEL)
    cos_t, sin_t = _rope_tables(pos)
    if cos_t.shape[0] < tm:
        cos_t = jnp.tile(cos_t, (tm // t, 1))
        sin_t = jnp.tile(sin_t, (tm // t, 1))
    shp = (n, t, D_A)
    conv_prm = (prm["conv_w"], prm["conv_b"], prm["ln_g"], prm["ln_b"])
    if kv_cache is None:
        assert t_valid == t and t % tm == 0
        ya, hist, q, k, v, gb, *kv_t = _even_inproj(x2d, prm["pre0"], prm["w_in_ab"], cos_t, sin_t, tm,
                                                    conv=(t, conv_hist) + conv_prm)
        yb = _attn_prompt(q.reshape(shp), k.reshape(shp), v.reshape(shp), gb.reshape(shp))
    else:
        kv_t = None
        u, ga, q, k, v, gb = _even_inproj(x2d, prm["pre0"], prm["w_in_ab"], cos_t, sin_t, tm)
        ya, hist = _conv_module(u.reshape(shp), ga.reshape(shp), conv_hist, *conv_prm, tc, t_valid)
        yb = _attn_sample(q.reshape(shp), k.reshape(shp), v.reshape(shp), kv_cache[0], kv_cache[1],
                          gb.reshape(shp), t_valid)
    x1 = _outproj([ya.reshape(rows, D_A), yb.reshape(rows, D_B)], prm["w_out_ab"], prm["post0"], x2d, tm_wide)
    qc, lf, kc, vc, gt = _odd_inproj(x1, prm["pre1"], prm["w_in_c"], prm["hgrn_lb"], 1, tm)
    wide = (n, t, D_C)
    oc, s_fin = _hgrn(qc.reshape(wide), lf.reshape(wide), kc.reshape(wide), vc.reshape(wide),
                      gt.reshape(wide), prm["gnorm"], state0, th, t_valid, hb)
    x2 = _outproj([oc.reshape(rows, D_C)], prm["w_out_c"], prm["post1"], x1, tm_wide)
    if kv_t is None:
        kv_t = (k.reshape(shp), v.reshape(shp))
    return x2.reshape(n, t, D_MODEL), hist, kv_t[0], kv_t[1], s_fin


def kernel(x_prompt, x_sample, cache_conv, cache_swa_k, cache_swa_v, state_hgrn, pre_norm, post_norm,
           w_in_ab, w_out_ab, conv_w, conv_b, conv_ln_g, conv_ln_b, w_in_c, w_out_c, hgrn_gnorm, hgrn_lb):
    n_p, seq_p, _ = x_prompt.shape
    n_s, seq_s, _ = x_sample.shape
    hist_rows = CONV_WIDTH - 1
    prm = dict(pre0=pre_norm[0], pre1=pre_norm[1], post0=post_norm[0], post1=post_norm[1],
               w_in_ab=w_in_ab[0].astype(BF16), w_out_ab=w_out_ab[0].astype(BF16),
               conv_w=conv_w[0], conv_b=conv_b[0], ln_g=conv_ln_g[0], ln_b=conv_ln_b[0],
               w_in_c=w_in_c[0].astype(BF16), w_out_c=w_out_c[0].astype(BF16),
               gnorm=hgrn_gnorm[0], hgrn_lb=hgrn_lb)

    pos_p = jnp.arange(seq_p, dtype=jnp.int32)
    yp, hist_p, k_p, v_p, s_p = _trunk(
        x_prompt, pos_p, jnp.zeros((n_p, HALO, D_A), F32), None,
        jnp.zeros((n_p, N_HEADS_C, HGRN_HEAD, HGRN_HEAD), F32), seq_p, prm)

    pad = SAMPLE_PAD - seq_s
    xs = jnp.pad(x_sample, ((0, 0), (0, pad), (0, 0)))
    pos_s = PAST_LEN + jnp.arange(SAMPLE_PAD, dtype=jnp.int32)
    hist_s = jnp.pad(cache_conv[0], ((0, 0), (HALO - hist_rows, 0), (0, 0)))
    kv = (jnp.transpose(cache_swa_k[0], (0, 2, 3, 1)), jnp.transpose(cache_swa_v[0], (0, 2, 3, 1)))
    ys, hist_s, k_s, v_s, s_s = _trunk(
        xs, pos_s, hist_s, kv, state_hgrn[0], seq_s, prm)

    heads = (N_HEADS_B, HEAD_DIM_B)
    return (yp, ys[:, :seq_s],
            hist_p[None, :, HALO - hist_rows:], hist_s[None, :, HALO - hist_rows:],
            jnp.transpose(k_p, (0, 3, 1, 2))[None], jnp.transpose(v_p, (0, 3, 1, 2))[None],
            k_s[:, :seq_s].reshape(1, n_s, seq_s, *heads), v_s[:, :seq_s].reshape(1, n_s, seq_s, *heads),
            s_p[None], s_s[None])
```
